```python
import jax, jax.numpy as jnp
from jax import lax
import numpy as np

D_MODEL = 2048
BATCH = 2
SEQ = 16384
DEPTH = 2

EPS = 1e-6
N_MIXERS = 2
RET_HEADS = 8
RET_DK = D_MODEL // RET_HEADS
RET_DV = 2 * RET_DK
RET_QK = RET_HEADS * RET_DK
RET_V = RET_HEADS * RET_DV
RET_PROJ = 2 * RET_QK + 2 * RET_V
RET_CHUNK = 128
RET_THETA = 10000.0
ATT_HEADS = 16
ATT_HEAD_DIM = D_MODEL // ATT_HEADS
ATT_WIDTH = ATT_HEADS * ATT_HEAD_DIM
ATT_GROUPS = ((128, 1), (512, 4), (2048, 16))
ATT_PROJ = len(ATT_GROUPS) * 3 * ATT_WIDTH
ATT_BLOCK = 128
ROT_DIM = ATT_HEAD_DIM // 4
ROPE_THETA = 500000.0
D_FF = 4 * D_MODEL

kernel_name = "hybrid_retention_dilated_attn_block"


def rms_norm(x, gain):
    x32 = x.astype(jnp.float32)
    y = x32 * lax.rsqrt(jnp.mean(x32 * x32, axis=-1, keepdims=True) + EPS)
    return (y * gain.astype(jnp.float32)).astype(x.dtype)


def modulate(h, shift, scale):
    return h * (1 + scale[:, None, :]) + shift[:, None, :]


def rope(x, positions, rot_dim, theta):
    half = rot_dim // 2
    inv_freq = jnp.power(jnp.float32(theta), -jnp.arange(half, dtype=jnp.float32) * 2.0 / rot_dim)
    ang = positions.astype(jnp.float32)[..., None] * inv_freq
    cos = jnp.cos(ang)[:, :, None, :]
    sin = jnp.sin(ang)[:, :, None, :]
    xr = x[..., :rot_dim].astype(jnp.float32)
    x1, x2 = xr[..., :half], xr[..., half:]
    rot = jnp.concatenate([x1 * cos - x2 * sin, x2 * cos + x1 * sin], axis=-1)
    return jnp.concatenate([rot.astype(x.dtype), x[..., rot_dim:]], axis=-1)


def retention_mixer(h, w_in, w_out, positions):
    B, S, _ = h.shape
    proj = h @ w_in
    q, k, v, g = jnp.split(proj, [RET_QK, 2 * RET_QK, 2 * RET_QK + RET_V], axis=-1)
    q = rope(q.reshape(B, S, RET_HEADS, RET_DK), positions, RET_DK, RET_THETA)
    k = rope(k.reshape(B, S, RET_HEADS, RET_DK), positions, RET_DK, RET_THETA) * (RET_DK ** -0.5)
    v = v.reshape(B, S, RET_HEADS, RET_DV)
    C = RET_CHUNK
    N = S // C

    def chunks(a):
        return a.astype(jnp.float32).reshape(B, N, C, RET_HEADS, -1).transpose(1, 0, 3, 2, 4)

    log_gamma = jnp.log1p(-jnp.exp2(-5.0 - jnp.arange(RET_HEADS, dtype=jnp.float32)))
    idx = jnp.arange(C, dtype=jnp.float32)
    diff = idx[:, None] - idx[None, :]
    decay = jnp.where(diff[None] >= 0,
                      jnp.exp(jnp.maximum(diff, 0.0)[None] * log_gamma[:, None, None]), 0.0)
    xi = jnp.exp((idx + 1.0)[None, :] * log_gamma[:, None])
    zeta = jnp.exp((C - 1.0 - idx)[None, :] * log_gamma[:, None])
    chunk_decay = jnp.exp(C * log_gamma)

    def step(state, qkv):
        qc, kc, vc = qkv
        scores = jnp.einsum('bhid,bhjd->bhij', qc, kc) * decay[None]
        inner = jnp.einsum('bhij,bhje->bhie', scores, vc)
        cross = jnp.einsum('bhid,bhde->bhie', qc, state) * xi[None, :, :, None]
        state = chunk_decay[None, :, None, None] * state + jnp.einsum(
            'bhjd,bhje->bhde', kc * zeta[None, :, :, None], vc)
        return state, inner + cross

    state0 = jnp.zeros((B, RET_HEADS, RET_DK, RET_DV), jnp.float32)
    _, o = lax.scan(step, state0, (chunks(q), chunks(k), chunks(v)))
    o = o.transpose(1, 0, 3, 2, 4).reshape(B, S, RET_HEADS, RET_DV)
    mu = jnp.mean(o, axis=-1, keepdims=True)
    var = jnp.mean(jnp.square(o - mu), axis=-1, keepdims=True)
    on = ((o - mu) * lax.rsqrt(var + EPS)).reshape(B, S, RET_V).astype(h.dtype)
    return (jax.nn.silu(g) * on) @ w_out


def dilated_window_group(q, k, v, dilation, n_back):
    B, S, H, dh = q.shape
    L = S // dilation
    nb = -(-L // ATT_BLOCK)
    Lp = nb * ATT_BLOCK

    def to_blocks(a):
        a = a.reshape(B, L, dilation, H, dh).transpose(0, 2, 3, 1, 4)
        a = jnp.pad(a, ((0, 0), (0, 0), (0, 0), (0, Lp - L), (0, 0)))
        return a.reshape(B, dilation, H, nb, ATT_BLOCK, dh)

    def with_prev(a):
        prev = jnp.pad(a, ((0, 0), (0, 0), (0, 0), (1, 0), (0, 0), (0, 0)))[:, :, :, :-1]
        return jnp.concatenate([prev, a], axis=-2)

    qb = to_blocks(q)
    kw = with_prev(to_blocks(k))
    vw = with_prev(to_blocks(v))
    s = jnp.einsum('brhnid,brhnjd->brhnij', qb, kw,
                   preferred_element_type=jnp.float32) * (dh ** -0.5)
    i = jnp.arange(ATT_BLOCK)[:, None]
    j = jnp.arange(2 * ATT_BLOCK)[None, :]
    dist = ATT_BLOCK + i - j
    key_idx = jnp.arange(nb)[:, None, None] * ATT_BLOCK - ATT_BLOCK + j[None]
    valid = (dist >= 0)[None] & (dist <= n_back)[None] & (key_idx >= 0)
    s = jnp.where(valid, s, -jnp.inf)
    m = jnp.max(s, axis=-1, keepdims=True)
    p = jnp.exp(s - m)
    l = jnp.sum(p, axis=-1)
    o = jnp.einsum('brhnij,brhnjd->brhnid', p.astype(vw.dtype), vw,
                   preferred_element_type=jnp.float32) / l[..., None]
    lse = m[..., 0] + jnp.log(l)
    o = o.reshape(B, dilation, H, Lp, dh)[:, :, :, :L].transpose(0, 3, 1, 2, 4).reshape(B, S, H, dh)
    lse = lse.reshape(B, dilation, H, Lp)[..., :L].transpose(0, 3, 1, 2).reshape(B, S, H)
    return o, lse


def dilated_attention_mixer(h, w_in, w_out, positions):
    B, S, _ = h.shape
    proj = (h @ w_in).reshape(B, S, len(ATT_GROUPS), 3, ATT_HEADS, ATT_HEAD_DIM)
    outs, lses = [], []
    for gi, (window, dilation) in enumerate(ATT_GROUPS):
        q = rope(proj[:, :, gi, 0], positions, ROT_DIM, ROPE_THETA)
        k = rope(proj[:, :, gi, 1], positions, ROT_DIM, ROPE_THETA)
        o, lse = dilated_window_group(q, k, proj[:, :, gi, 2], dilation, window // dilation)
        outs.append(o)
        lses.append(lse)
    wts = jax.nn.softmax(jnp.stack(lses, axis=0), axis=0)
    o = jnp.sum(wts[..., None] * jnp.stack(outs, axis=0), axis=0)
    return o.reshape(B, S, ATT_WIDTH).astype(h.dtype) @ w_out


def squared_relu_mlp(h, w1, w2):
    return jnp.square(jax.nn.relu(h @ w1)) @ w2


def setup_inputs(seed: int = 0) -> dict:
    key = jax.random.key(seed)
    ks = jax.random.split(key, 16)
    n_ret = (DEPTH + 1) // 2
    n_att = DEPTH // 2
    f32 = jnp.float32
    x = jax.random.normal(ks[0], (BATCH, SEQ, D_MODEL), f32)
    c = jax.random.normal(ks[1], (BATCH, D_MODEL), f32)
    offsets = jax.random.randint(ks[2], (BATCH, 1), 0, 1024, dtype=jnp.int32)
    positions = (jnp.arange(SEQ, dtype=jnp.int32)[None, :] + offsets).astype(jnp.int32)
    ada_w = jax.random.normal(ks[3], (DEPTH, D_MODEL, 6 * D_MODEL), f32) * (0.5 * D_MODEL ** -0.5)
    ada_b = 0.02 * jax.random.normal(ks[4], (DEPTH, 6 * D_MODEL), f32)
    norm_mix = 1.0 + 0.02 * jax.random.normal(ks[5], (DEPTH, D_MODEL), f32)
    norm_mlp = 1.0 + 0.02 * jax.random.normal(ks[6], (DEPTH, D_MODEL), f32)
    ret_w_in = jax.random.normal(ks[7], (n_ret, D_MODEL, RET_PROJ), f32) * (D_MODEL ** -0.5)
    ret_w_out = jax.random.normal(ks[8], (n_ret, RET_V, D_MODEL), f32) * (RET_V ** -0.5)
    att_w_in = jax.random.normal(ks[9], (n_att, D_MODEL, ATT_PROJ), f32) * (D_MODEL ** -0.5)
    att_w_out = jax.random.normal(ks[10], (n_att, ATT_WIDTH, D_MODEL), f32) * (ATT_WIDTH ** -0.5)
    mlp_w1 = jax.random.normal(ks[11], (DEPTH, D_MODEL, D_FF), f32) * (D_MODEL ** -0.5)
    mlp_w2 = jax.random.normal(ks[12], (DEPTH, D_FF, D_MODEL), f32) * (D_FF ** -0.5)
    final_norm = 1.0 + 0.02 * jax.random.normal(ks[13], (D_MODEL,), f32)
    return {"x": x, "c": c, "positions": positions, "ada_w": ada_w, "ada_b": ada_b,
            "norm_mix": norm_mix, "norm_mlp": norm_mlp, "ret_w_in": ret_w_in,
            "ret_w_out": ret_w_out, "att_w_in": att_w_in, "att_w_out": att_w_out,
            "mlp_w1": mlp_w1, "mlp_w2": mlp_w2, "final_norm": final_norm}


def reference(x, c, positions, ada_w, ada_b, norm_mix, norm_mlp, ret_w_in, ret_w_out,
              att_w_in, att_w_out, mlp_w1, mlp_w2, final_norm):
    c_act = jax.nn.silu(c)
    for i in range(DEPTH):
        mod = c_act @ ada_w[i] + ada_b[i]
        sh1, sc1, g1, sh2, sc2, g2 = jnp.split(mod, 6, axis=-1)
        h = modulate(rms_norm(x, norm_mix[i]), sh1, sc1)
        if i % N_MIXERS == 0:
            y = retention_mixer(h, ret_w_in[i // N_MIXERS], ret_w_out[i // N_MIXERS], positions)
        else:
            y = dilated_attention_mixer(h, att_w_in[i // N_MIXERS], att_w_out[i // N_MIXERS], positions)
        x = x + g1[:, None, :] * y
        h = modulate(rms_norm(x, norm_mlp[i]), sh2, sc2)
        x = x + g2[:, None, :] * squared_relu_mlp(h, mlp_w1[i], mlp_w2[i])
    return rms_norm(x, final_norm)
```

```python
import functools

import numpy as np
import jax
import jax.numpy as jnp
from jax import lax
from jax.experimental import pallas as pl
from jax.experimental.pallas import tpu as pltpu

F32 = jnp.float32
BF16 = jnp.bfloat16

EPS = 1e-6
D_MODEL = 2048
D_FF = 4 * D_MODEL
RET_HEADS = 8
RET_DK = D_MODEL // RET_HEADS
RET_DV = 2 * RET_DK
RET_QK = RET_HEADS * RET_DK
RET_V = RET_HEADS * RET_DV
RET_PROJ = 2 * RET_QK + 2 * RET_V
RET_THETA = 10000.0
RET_CHUNK = 256
ATT_HEADS = 16
ATT_HEAD_DIM = D_MODEL // ATT_HEADS
ATT_WIDTH = ATT_HEADS * ATT_HEAD_DIM
ATT_GROUPS = ((128, 1), (512, 4), (2048, 16))
ATT_BLOCK = 128
ROT_DIM = ATT_HEAD_DIM // 4
ROPE_THETA = 500000.0
LANES = 128

VMEM_LIMIT_BYTES = 56 * 1024 * 1024


def _params(*sem):
    return pltpu.CompilerParams(dimension_semantics=sem, vmem_limit_bytes=VMEM_LIMIT_BYTES)


def _dot(a, b):
    return jnp.dot(a, b, preferred_element_type=F32)


def _dot_nt(a, b):
    return lax.dot_general(a, b, (((1,), (1,)), ((), ())), preferred_element_type=F32)


def _dot_tn(a, b):
    return lax.dot_general(a, b, (((0,), (0,)), ((), ())), preferred_element_type=F32)


def _norm_modulate(x, gain, shift, scale):
    ms = jnp.mean(x * x, axis=-1, keepdims=True)
    y = x * lax.rsqrt(ms + EPS) * gain
    return y * (1.0 + scale) + shift


def _ada_kernel(ct_ref, w_ref, b_ref, o_ref):
    ct = ct_ref[...]
    cs = ct * jax.nn.sigmoid(ct)
    w = w_ref[...]
    for b in range(o_ref.shape[0]):
        o_ref[b:b + 1, :] = jnp.sum(w * cs[:, b:b + 1], axis=0, keepdims=True) + b_ref[...]


def _ada(c, ada_w, ada_b, tn=1024):
    depth, d, n = ada_w.shape
    bsz = c.shape[0]
    return pl.pallas_call(
        _ada_kernel,
        grid=(depth, n // tn),
        in_specs=[
            pl.BlockSpec((d, bsz), lambda l, j: (0, 0)),
            pl.BlockSpec((None, d, tn), lambda l, j: (l, 0, j)),
            pl.BlockSpec((None, 1, tn), lambda l, j: (l, 0, j)),
        ],
        out_specs=pl.BlockSpec((None, bsz, tn), lambda l, j: (l, 0, j)),
        out_shape=jax.ShapeDtypeStruct((depth, bsz, n), F32),
        compiler_params=_params("parallel", "parallel"),
        name="ada_mod",
    )(c.T, ada_w, ada_b.reshape(depth, 1, n))


def _mod_spec(layer, which):
    return pl.BlockSpec((None, None, None, 1, D_MODEL),
                        lambda b, *_: (layer, b, which, 0, 0))


def _ret_proj_kernel(x_ref, pos_ref, gain_ref, shift_ref, scale_ref, invf_ref, w_ref,
                     o_ref, h_ref, cos_ref, sin_ref, *, tn):
    j = pl.program_id(2)

    @pl.when(j == 0)
    def _():
        h = _norm_modulate(x_ref[...], gain_ref[...], shift_ref[...], scale_ref[...])
        h_ref[...] = h.astype(BF16)
        ang = pos_ref[...].astype(F32) * invf_ref[...]
        cos_ref[...] = jnp.cos(ang)
        sin_ref[...] = jnp.sin(ang)

    acc = _dot(h_ref[...], w_ref[...])
    n_qk = 2 * RET_QK // tn

    @pl.when(j < n_qk)
    def _():
        kscale = jnp.where(j >= n_qk // 2, RET_DK ** -0.5, 1.0).astype(F32)
        c = cos_ref[...] * kscale
        s = sin_ref[...] * kscale
        half = RET_DK // 2
        for hh in range(tn // RET_DK):
            lo = hh * RET_DK
            a1 = acc[:, lo:lo + half]
            a2 = acc[:, lo + half:lo + RET_DK]
            o_ref[:, lo:lo + half] = (a1 * c - a2 * s).astype(BF16)
            o_ref[:, lo + half:lo + RET_DK] = (a2 * c + a1 * s).astype(BF16)

    @pl.when(j >= n_qk)
    def _():
        o_ref[...] = acc.astype(BF16)


def _ret_proj(x, pos3, mod, layer, gain, w, tm=1024, tn=1024):
    bsz, s, d = x.shape
    n = w.shape[1]
    half = RET_DK // 2
    invf = np.power(np.float32(RET_THETA),
                    -np.arange(half, dtype=np.float32) * np.float32(2.0) / np.float32(RET_DK))
    invf = jnp.asarray(invf.astype(np.float32).reshape(1, half))
    return pl.pallas_call(
        functools.partial(_ret_proj_kernel, tn=tn),
        grid=(bsz, s // tm, n // tn),
        in_specs=[
            pl.BlockSpec((None, tm, d), lambda b, i, j: (b, i, 0)),
            pl.BlockSpec((None, tm, 1), lambda b, i, j: (b, i, 0)),
            pl.BlockSpec((1, d), lambda b, i, j: (0, 0)),
            _mod_spec(layer, 0),
            _mod_spec(layer, 1),
            pl.BlockSpec((1, half), lambda b, i, j: (0, 0)),
            pl.BlockSpec((d, tn), lambda b, i, j: (0, j)),
        ],
        out_specs=pl.BlockSpec((None, tm, tn), lambda b, i, j: (b, i, j)),
        out_shape=jax.ShapeDtypeStruct((bsz, s, n), BF16),
        scratch_shapes=[pltpu.VMEM((tm, d), BF16), pltpu.VMEM((tm, half), F32),
                        pltpu.VMEM((tm, half), F32)],
        compiler_params=_params("parallel", "parallel", "arbitrary"),
        name="ret_proj",
    )(x, pos3, gain, mod, mod, invf, w)


def _ret_tables(chunk):
    lg = np.log1p(-np.exp2(-5.0 - np.arange(RET_HEADS, dtype=np.float64)))
    idx = np.arange(chunk, dtype=np.float64)
    diff = idx[:, None] - idx[None, :]
    decay = np.where(diff[None] >= 0, np.exp(np.maximum(diff, 0.0)[None] * lg[:, None, None]), 0.0)
    xi = np.exp((idx + 1.0)[None, :] * lg[:, None])
    zeta = np.exp((chunk - 1.0 - idx)[None, :] * lg[:, None])
    cd = np.exp(chunk * lg)
    return (jnp.asarray(decay.astype(np.float32)),
            jnp.asarray(xi.astype(np.float32)[:, :, None]),
            jnp.asarray(zeta.astype(np.float32)[:, :, None]),
            [float(v) for v in cd])


def _ret_kernel(q_ref, k_ref, v_ref, g_ref, decay_ref, xi_ref, zeta_ref, o_ref, state_ref,
                *, tt, chunk, cd):
    @pl.when(pl.program_id(1) == 0)
    def _():
        state_ref[...] = jnp.zeros_like(state_ref)

    for h in range(RET_HEADS):
        qs = slice(h * RET_DK, (h + 1) * RET_DK)
        vs = slice(h * RET_DV, (h + 1) * RET_DV)
        for c in range(tt // chunk):
            rows = slice(c * chunk, (c + 1) * chunk)
            qc = q_ref[rows, qs]
            kc = k_ref[rows, qs]
            vc = v_ref[rows, vs]
            st = state_ref[h]
            scores = _dot_nt(qc, kc) * decay_ref[h]
            inner = _dot(scores.astype(BF16), vc)
            cross = _dot(qc, st.astype(BF16)) * xi_ref[h]
            o = inner + cross
            kz = (kc.astype(F32) * zeta_ref[h]).astype(BF16)
            state_ref[h] = cd[h] * st + _dot_tn(kz, vc)
            mu = jnp.mean(o, axis=-1, keepdims=True)
            oc = o - mu
            var = jnp.mean(oc * oc, axis=-1, keepdims=True)
            on = oc * lax.rsqrt(var + EPS)
            g = g_ref[rows, vs].astype(F32)
            o_ref[rows, vs] = (g * jax.nn.sigmoid(g) * on).astype(BF16)


def _retention(proj, tt=512, chunk=RET_CHUNK):
    bsz, s, _ = proj.shape
    decay, xi, zeta, cd = _ret_tables(chunk)
    const3 = lambda b, n: (0, 0, 0)
    return pl.pallas_call(
        functools.partial(_ret_kernel, tt=tt, chunk=chunk, cd=cd),
        grid=(bsz, s // tt),
        in_specs=[
            pl.BlockSpec((None, tt, RET_QK), lambda b, n: (b, n, 0)),
            pl.BlockSpec((None, tt, RET_QK), lambda b, n: (b, n, 1)),
            pl.BlockSpec((None, tt, RET_V), lambda b, n: (b, n, 1)),
            pl.BlockSpec((None, tt, RET_V), lambda b, n: (b, n, 2)),
            pl.BlockSpec((RET_HEADS, chunk, chunk), const3),
            pl.BlockSpec((RET_HEADS, chunk, 1), const3),
            pl.BlockSpec((RET_HEADS, chunk, 1), const3),
        ],
        out_specs=pl.BlockSpec((None, tt, RET_V), lambda b, n: (b, n, 0)),
        out_shape=jax.ShapeDtypeStruct((bsz, s, RET_V), BF16),
        scratch_shapes=[pltpu.VMEM((RET_HEADS, RET_DK, RET_DV), F32)],
        compiler_params=_params("parallel", "arbitrary"),
        name="retention",
    )(proj, proj, proj, proj, decay, xi, zeta)


def _out_proj_kernel(a_ref, w_ref, x_ref, gate_ref, o_ref):
    o_ref[...] = x_ref[...] + gate_ref[...] * _dot(a_ref[...], w_ref[...])


def _out_proj(a, w, x, mod, layer, tm=512, tn=1024):
    bsz, s, k = a.shape
    d = w.shape[1]
    return pl.pallas_call(
        _out_proj_kernel,
        grid=(d // tn, bsz, s // tm),
        in_specs=[
            pl.BlockSpec((None, tm, k), lambda j, b, i: (b, i, 0)),
            pl.BlockSpec((k, tn), lambda j, b, i: (0, j)),
            pl.BlockSpec((None, tm, tn), lambda j, b, i: (b, i, j)),
            pl.BlockSpec((None, None, None, 1, tn), lambda j, b, i: (layer, b, 2, 0, j)),
        ],
        out_specs=pl.BlockSpec((None, tm, tn), lambda j, b, i: (b, i, j)),
        out_shape=jax.ShapeDtypeStruct((bsz, s, d), F32),
        compiler_params=_params("parallel", "parallel", "parallel"),
        name="out_proj",
    )(a, w, x, mod)


def _mlp_kernel(x_ref, gain_ref, shift_ref, scale_ref, gate_ref, w1_ref, w2_ref, fgain_ref,
                o_ref, h_ref, acc_ref, *, final_norm):
    k = pl.program_id(2)

    @pl.when(k == 0)
    def _():
        h = _norm_modulate(x_ref[...], gain_ref[...], shift_ref[...], scale_ref[...])
        h_ref[...] = h.astype(BF16)

    a = jnp.maximum(_dot(h_ref[...], w1_ref[...]), 0.0)
    contrib = _dot((a * a).astype(BF16), w2_ref[...])

    @pl.when(k == 0)
    def _():
        acc_ref[...] = contrib

    @pl.when(k > 0)
    def _():
        acc_ref[...] += contrib

    @pl.when(k == pl.num_programs(2) - 1)
    def _():
        y = x_ref[...] + gate_ref[...] * acc_ref[...]
        if final_norm:
            ms = jnp.mean(y * y, axis=-1, keepdims=True)
            y = y * lax.rsqrt(ms + EPS) * fgain_ref[...]
        o_ref[...] = y


def _mlp(x, mod, layer, gain, w1, w2, fgain, final_norm, tm=512, tk=1024):
    bsz, s, d = x.shape
    dff = w1.shape[1]
    row = lambda b, i, k: (0, 0)
    return pl.pallas_call(
        functools.partial(_mlp_kernel, final_norm=final_norm),
        grid=(bsz, s // tm, dff // tk),
        in_specs=[
            pl.BlockSpec((None, tm, d), lambda b, i, k: (b, i, 0)),
            pl.BlockSpec((1, d), row),
            _mod_spec(layer, 3),
            _mod_spec(layer, 4),
            _mod_spec(layer, 5),
            pl.BlockSpec((d, tk), lambda b, i, k: (0, k)),
            pl.BlockSpec((tk, d), lambda b, i, k: (k, 0)),
            pl.BlockSpec((1, d), row),
        ],
        out_specs=pl.BlockSpec((None, tm, d), lambda b, i, k: (b, i, 0)),
        out_shape=jax.ShapeDtypeStruct((bsz, s, d), F32),
        scratch_shapes=[pltpu.VMEM((tm, d), BF16), pltpu.VMEM((tm, d), F32)],
        compiler_params=_params("parallel", "parallel", "arbitrary"),
        name="mlp",
    )(x, gain, mod, mod, mod, w1, w2, fgain)


def _perm_matrix(tm, dil):
    t = np.arange(tm)
    p = (t % dil) * (tm // dil) + t // dil
    m = np.zeros((tm, tm), np.float32)
    m[p, t] = 1.0
    return m


def _att_proj_kernel(x_ref, pos1_ref, pos4_ref, pos16_ref, gain_ref, shift_ref, scale_ref,
                     invf_ref, p4_ref, p16_ref, w_ref, o1_ref, o4_ref, o16_ref,
                     h_ref, tab_ref, *, tm, tn):
    j = pl.program_id(2)
    tiles = 3 * ATT_WIDTH // tn
    pos_refs = (pos1_ref, pos4_ref, pos16_ref)
    out_refs = (o1_ref, o4_ref, o16_ref)

    @pl.when(j == 0)
    def _():
        h = _norm_modulate(x_ref[...], gain_ref[...], shift_ref[...], scale_ref[...]).astype(BF16)
        h_ref[0] = h
        h_ref[1] = _dot(p4_ref[...], h).astype(BF16)
        h_ref[2] = _dot(p16_ref[...], h).astype(BF16)
        lane = lax.broadcasted_iota(jnp.int32, (tm, LANES), 1)
        half = ROT_DIM // 2
        for gi in range(3):
            pos = pos_refs[gi][...].reshape(tm, 1).astype(F32)
            ang = pos * invf_ref[...]
            cos = jnp.cos(ang)
            sin = jnp.sin(ang)
            tab_ref[gi, 0] = cos
            tab_ref[gi, 1] = jnp.where(lane < half, -sin, 0.0)
            tab_ref[gi, 2] = jnp.where((lane >= half) & (lane < ROT_DIM), sin, 0.0)

    for gi, (_, dil) in enumerate(ATT_GROUPS):
        @pl.when(j // tiles == gi)
        def _(gi=gi, dil=dil):
            acc = _dot(h_ref[gi], w_ref[...])
            jj = j - gi * tiles
            out = out_refs[gi]
            n_rot = 2 * ATT_WIDTH // tn

            @pl.when(jj < n_rot)
            def _():
                qscale = jnp.where(jj < n_rot // 2, ATT_HEAD_DIM ** -0.5, 1.0).astype(F32)
                c = tab_ref[gi, 0] * qscale
                sa = tab_ref[gi, 1] * qscale
                sb = tab_ref[gi, 2] * qscale
                half = ROT_DIM // 2
                for hh in range(tn // ATT_HEAD_DIM):
                    blk = acc[:, hh * LANES:(hh + 1) * LANES]
                    rot = (blk * c + pltpu.roll(blk, LANES - half, 1) * sa
                           + pltpu.roll(blk, half, 1) * sb)
                    out[:, :, hh * LANES:(hh + 1) * LANES] = (
                        rot.astype(BF16).reshape(dil, tm // dil, LANES))

            @pl.when(jj >= n_rot)
            def _():
                out[...] = acc.astype(BF16).reshape(dil, tm // dil, tn)


def _att_proj(x, positions, mod, layer, gain, w, tm=512, tn=1024):
    bsz, s, d = x.shape
    tiles = 3 * ATT_WIDTH // tn
    half = ROT_DIM // 2
    invf = np.power(np.float32(ROPE_THETA),
                    -np.arange(half, dtype=np.float32) * np.float32(2.0) / np.float32(ROT_DIM))
    invf_full = np.zeros((1, LANES), np.float32)
    invf_full[0, :half] = invf
    invf_full[0, half:ROT_DIM] = invf
    in_specs = [pl.BlockSpec((None, tm, d), lambda b, i, j: (b, i, 0))]
    operands = [x]
    out_specs, out_shapes = [], []
    for gi, (_, dil) in enumerate(ATT_GROUPS):
        ln = s // dil
        operands.append(positions.reshape(bsz, ln, dil).transpose(0, 2, 1)[..., None])
        in_specs.append(pl.BlockSpec((None, dil, tm // dil, 1), lambda b, i, j: (b, 0, i, 0)))
        out_specs.append(pl.BlockSpec(
            (None, dil, tm // dil, tn),
            lambda b, i, j, gi=gi: (b, 0, i, jnp.clip(j - gi * tiles, 0, tiles - 1))))
        out_shapes.append(jax.ShapeDtypeStruct((bsz, dil, ln, 3 * ATT_WIDTH), BF16))
    const2 = lambda b, i, j: (0, 0)
    in_specs += [
        pl.BlockSpec((1, d), const2),
        _mod_spec(layer, 0),
        _mod_spec(layer, 1),
        pl.BlockSpec((1, LANES), const2),
        pl.BlockSpec((tm, tm), const2),
        pl.BlockSpec((tm, tm), const2),
        pl.BlockSpec((d, tn), lambda b, i, j: (0, j)),
    ]
    operands += [gain, mod, mod, jnp.asarray(invf_full),
                 jnp.asarray(_perm_matrix(tm, 4), BF16), jnp.asarray(_perm_matrix(tm, 16), BF16), w]
    return pl.pallas_call(
        functools.partial(_att_proj_kernel, tm=tm, tn=tn),
        grid=(bsz, s // tm, 3 * tiles),
        in_specs=in_specs,
        out_specs=out_specs,
        out_shape=out_shapes,
        scratch_shapes=[pltpu.VMEM((3, tm, d), BF16), pltpu.VMEM((3, 3, tm, LANES), F32)],
        compiler_params=_params("parallel", "parallel", "arbitrary"),
        name="att_proj",
    )(*operands)


def _att_kernel(q_ref, kc_ref, kp_ref, vc_ref, vp_ref, o_ref, st_ref, *, tq):
    n = pl.program_id(2)
    blk = ATT_BLOCK
    row = lax.broadcasted_iota(jnp.int32, (blk, 2 * blk), 0)
    col = lax.broadcasted_iota(jnp.int32, (blk, 2 * blk), 1)
    band = (col >= row) & (col <= row + blk)
    band_first = band & ((col >= blk) | (n > 0))
    lane = lax.broadcasted_iota(jnp.int32, (blk, LANES), 1)
    for i in range(tq // blk):
        rows = slice(i * blk, (i + 1) * blk)
        stats = jnp.zeros((blk, LANES), F32)
        for h in range(ATT_HEADS):
            hs = slice(h * ATT_HEAD_DIM, (h + 1) * ATT_HEAD_DIM)
            qi = q_ref[rows, hs]
            if i == 0:
                kk = jnp.concatenate([kp_ref[:, hs], kc_ref[0:blk, hs]], axis=0)
                vv = jnp.concatenate([vp_ref[:, hs], vc_ref[0:blk, hs]], axis=0)
                mask = band_first
            else:
                kk = kc_ref[(i - 1) * blk:(i + 1) * blk, hs]
                vv = vc_ref[(i - 1) * blk:(i + 1) * blk, hs]
                mask = band
            s = jnp.where(mask, _dot_nt(qi, kk), -jnp.inf)
            m = jnp.max(s, axis=-1, keepdims=True)
            p = jnp.exp(s - m)
            l = jnp.sum(p, axis=-1, keepdims=True)
            o = _dot(p.astype(BF16), vv) / l
            o_ref[rows, hs] = o.astype(BF16)
            stats = jnp.where(lane == h, m + jnp.log(l), stats)
        hi = stats.astype(BF16)
        r1 = stats - hi.astype(F32)
        mid = r1.astype(BF16)
        lo = (r1 - mid.astype(F32)).astype(BF16)
        st_ref[rows, 0:LANES] = hi
        st_ref[rows, LANES:2 * LANES] = mid
        st_ref[rows, 2 * LANES:3 * LANES] = lo


def _attention(qkv, tq=256):
    bsz, dil, ln, _ = qkv.shape
    tq = min(tq, ln)
    per = tq // ATT_BLOCK
    cur = lambda c: pl.BlockSpec((None, None, tq, ATT_WIDTH), lambda b, r, n: (b, r, n, c))
    prev = lambda c: pl.BlockSpec((None, None, ATT_BLOCK, ATT_WIDTH),
                                  lambda b, r, n: (b, r, jnp.maximum(n * per - 1, 0), c))
    return pl.pallas_call(
        functools.partial(_att_kernel, tq=tq),
        grid=(bsz, dil, ln // tq),
        in_specs=[cur(0), cur(1), prev(1), cur(2), prev(2)],
        out_specs=[pl.BlockSpec((None, None, tq, ATT_WIDTH), lambda b, r, n: (b, r, n, 0)),
                   pl.BlockSpec((None, None, tq, 3 * LANES), lambda b, r, n: (b, r, n, 0))],
        out_shape=[jax.ShapeDtypeStruct((bsz, dil, ln, ATT_WIDTH), BF16),
                   jax.ShapeDtypeStruct((bsz, dil, ln, 3 * LANES), BF16)],
        compiler_params=_params("parallel", "parallel", "parallel"),
        name=f"attention_d{dil}",
    )(qkv, qkv, qkv, qkv, qkv)


def _att_merge_kernel(o1_ref, o4_ref, o16_ref, s1_ref, s4_ref, s16_ref, pt4_ref, pt16_ref,
                      ex_ref, w_ref, x_ref, gate_ref, out_ref, lhs_ref, *, tm):
    def lse_of(pieces):
        return pieces[:, 0:LANES] + pieces[:, LANES:2 * LANES] + pieces[:, 2 * LANES:3 * LANES]

    lse1 = lse_of(s1_ref[...].astype(F32))
    lse4 = lse_of(_dot(pt4_ref[...], s4_ref[...].reshape(tm, 3 * LANES)))
    lse16 = lse_of(_dot(pt16_ref[...], s16_ref[...].reshape(tm, 3 * LANES)))
    m = jnp.maximum(jnp.maximum(lse1, lse4), lse16)
    e1 = jnp.exp(lse1 - m)
    e4 = jnp.exp(lse4 - m)
    e16 = jnp.exp(lse16 - m)
    inv = 1.0 / (e1 + e4 + e16)

    def split(w):
        hi = w.astype(BF16)
        lo = (w - hi.astype(F32)).astype(BF16)
        return jnp.concatenate([hi, lo], axis=1)

    w1, w4, w16 = split(e1 * inv), split(e4 * inv), split(e16 * inv)
    cw = 4 * ATT_HEAD_DIM
    for cb in range(ATT_WIDTH // cw):
        cs = slice(cb * cw, (cb + 1) * cw)
        ex = ex_ref[:, cs]
        a1 = o1_ref[:, cs].astype(F32)
        a4 = _dot(pt4_ref[...], o4_ref[:, :, cs].reshape(tm, cw))
        a16 = _dot(pt16_ref[...], o16_ref[:, :, cs].reshape(tm, cw))
        mix = _dot(w1, ex) * a1 + _dot(w4, ex) * a4 + _dot(w16, ex) * a16
        lhs_ref[:, cs] = mix.astype(BF16)
    out_ref[...] = x_ref[...] + gate_ref[...] * _dot(lhs_ref[...], w_ref[...])


def _att_merge(outs, stats, w, x, mod, layer, tm=256):
    bsz, s, d = x.shape
    in_specs, operands = [], []
    for arr in list(outs) + list(stats):
        dil, width = arr.shape[1], arr.shape[3]
        if dil == 1:
            in_specs.append(pl.BlockSpec((None, None, tm, width), lambda b, i: (b, 0, i, 0)))
        else:
            in_specs.append(pl.BlockSpec((None, dil, tm // dil, width), lambda b, i: (b, 0, i, 0)))
        operands.append(arr)
    expand = np.zeros((2 * LANES, ATT_WIDTH), np.float32)
    for h in range(ATT_HEADS):
        expand[h, h * ATT_HEAD_DIM:(h + 1) * ATT_HEAD_DIM] = 1.0
        expand[LANES + h, h * ATT_HEAD_DIM:(h + 1) * ATT_HEAD_DIM] = 1.0
    const2 = lambda b, i: (0, 0)
    in_specs += [
        pl.BlockSpec((tm, tm), const2),
        pl.BlockSpec((tm, tm), const2),
        pl.BlockSpec((2 * LANES, ATT_WIDTH), const2),
        pl.BlockSpec((ATT_WIDTH, d), const2),
        pl.BlockSpec((None, tm, d), lambda b, i: (b, i, 0)),
        _mod_spec(layer, 2),
    ]
    operands += [jnp.asarray(_perm_matrix(tm, 4).T, BF16), jnp.asarray(_perm_matrix(tm, 16).T, BF16),
                 jnp.asarray(expand, BF16), w, x, mod]
    return pl.pallas_call(
        functools.partial(_att_merge_kernel, tm=tm),
        grid=(bsz, s // tm),
        in_specs=in_specs,
        out_specs=pl.BlockSpec((None, tm, d), lambda b, i: (b, i, 0)),
        out_shape=jax.ShapeDtypeStruct((bsz, s, d), F32),
        scratch_shapes=[pltpu.VMEM((tm, ATT_WIDTH), BF16)],
        compiler_params=_params("parallel", "parallel"),
        name="att_merge",
    )(*operands)


def kernel(x, c, positions, ada_w, ada_b, norm_mix, norm_mlp, ret_w_in, ret_w_out,
           att_w_in, att_w_out, mlp_w1, mlp_w2, final_norm):
    bsz, s, d = x.shape
    depth = ada_w.shape[0]
    assert d == D_MODEL and depth == 2 and s % 2048 == 0
    mod = _ada(c, ada_w, ada_b).reshape(depth, bsz, 6, 1, d)
    pos3 = positions.reshape(bsz, s, 1)
    fgain = final_norm.reshape(1, d)

    proj = _ret_proj(x, pos3, mod, 0, norm_mix[0].reshape(1, d), ret_w_in[0].astype(BF16))
    ret = _retention(proj)
    x = _out_proj(ret, ret_w_out[0].astype(BF16), x, mod, 0)
    x = _mlp(x, mod, 0, norm_mlp[0].reshape(1, d), mlp_w1[0].astype(BF16), mlp_w2[0].astype(BF16),
             fgain, final_norm=False)

    qkvs = _att_proj(x, positions, mod, 1, norm_mix[1].reshape(1, d), att_w_in[0].astype(BF16))
    outs, stats = zip(*[_attention(qkv) for qkv in qkvs])
    x = _att_merge(outs, stats, att_w_out[0].astype(BF16), x, mod, 1)
    x = _mlp(x, mod, 1, norm_mlp[1].reshape(1, d), mlp_w1[1].astype(BF16), mlp_w2[1].astype(BF16),
             fgain, final_norm=True)
    return x
```

```python
import functools

import numpy as np
import jax
import jax.numpy as jnp
from jax import lax
from jax.experimental import pallas as pl
from jax.experimental.pallas import tpu as pltpu

F32 = jnp.float32
BF16 = jnp.bfloat16

EPS = 1e-6
D_MODEL = 2048
D_FF = 4 * D_MODEL
RET_HEADS = 8
RET_DK = D_MODEL // RET_HEADS
RET_DV = 2 * RET_DK
RET_QK = RET_HEADS * RET_DK
RET_V = RET_HEADS * RET_DV
RET_PROJ = 2 * RET_QK + 2 * RET_V
RET_THETA = 10000.0
RET_CHUNK = 256
ATT_HEADS = 16
ATT_HEAD_DIM = D_MODEL // ATT_HEADS
ATT_WIDTH = ATT_HEADS * ATT_HEAD_DIM
ATT_GROUPS = ((128, 1), (512, 4), (2048, 16))
ATT_BLOCK = 128
ROT_DIM = ATT_HEAD_DIM // 4
ROPE_THETA = 500000.0
LANES = 128

VMEM_LIMIT_BYTES = 56 * 1024 * 1024


def _params(*sem):
    return pltpu.CompilerParams(dimension_semantics=sem, vmem_limit_bytes=VMEM_LIMIT_BYTES)


def _dot(a, b):
    return jnp.dot(a, b, preferred_element_type=F32)


def _dot_nt(a, b):
    return lax.dot_general(a, b, (((1,), (1,)), ((), ())), preferred_element_type=F32)


def _dot_tn(a, b):
    return lax.dot_general(a, b, (((0,), (0,)), ((), ())), preferred_element_type=F32)


def _norm_modulate(x, gain, shift, scale):
    ms = jnp.mean(x * x, axis=-1, keepdims=True)
    y = x * lax.rsqrt(ms + EPS) * gain
    return y * (1.0 + scale) + shift


def _ada_kernel(ct_ref, w_ref, b_ref, o_ref):
    ct = ct_ref[...]
    cs = ct * jax.nn.sigmoid(ct)
    w = w_ref[...]
    for b in range(o_ref.shape[0]):
        o_ref[b:b + 1, :] = jnp.sum(w * cs[:, b:b + 1], axis=0, keepdims=True) + b_ref[...]


def _ada(c, ada_w, ada_b, tn=1024):
    depth, d, n = ada_w.shape
    bsz = c.shape[0]
    return pl.pallas_call(
        _ada_kernel,
        grid=(depth, n // tn),
        in_specs=[
            pl.BlockSpec((d, bsz), lambda l, j: (0, 0)),
            pl.BlockSpec((None, d, tn), lambda l, j: (l, 0, j)),
            pl.BlockSpec((None, 1, tn), lambda l, j: (l, 0, j)),
        ],
        out_specs=pl.BlockSpec((None, bsz, tn), lambda l, j: (l, 0, j)),
        out_shape=jax.ShapeDtypeStruct((depth, bsz, n), F32),
        compiler_params=_params("parallel", "parallel"),
        name="ada_mod",
    )(c.T, ada_w, ada_b.reshape(depth, 1, n))


def _mod_spec(layer, which):
    return pl.BlockSpec((None, None, None, 1, D_MODEL),
                        lambda b, *_: (layer, b, which, 0, 0))


def _ret_proj_kernel(x_ref, pos_ref, gain_ref, shift_ref, scale_ref, invf_ref, w_ref,
                     o_ref, h_ref, cos_ref, sin_ref, *, tn):
    j = pl.program_id(2)

    @pl.when(j == 0)
    def _():
        h = _norm_modulate(x_ref[...], gain_ref[...], shift_ref[...], scale_ref[...])
        h_ref[...] = h.astype(BF16)
        ang = pos_ref[...].astype(F32) * invf_ref[...]
        cos_ref[...] = jnp.cos(ang)
        sin_ref[...] = jnp.sin(ang)

    n_qk = 2 * RET_QK // tn
    kscale = jnp.where(j >= n_qk // 2, RET_DK ** -0.5, 1.0).astype(F32)
    rotated = j < n_qk
    c = jnp.where(rotated, cos_ref[...] * kscale, 1.0)
    s = jnp.where(rotated, sin_ref[...] * kscale, 0.0)
    half = RET_DK // 2
    h = h_ref[...]
    for hh in range(tn // RET_DK):
        lo = hh * RET_DK
        acc = _dot(h, w_ref[:, lo:lo + RET_DK])
        a1 = acc[:, :half]
        a2 = acc[:, half:]
        o_ref[:, lo:lo + half] = (a1 * c - a2 * s).astype(BF16)
        o_ref[:, lo + half:lo + RET_DK] = (a2 * c + a1 * s).astype(BF16)


def _ret_proj(x, pos3, mod, layer, gain, w, tm=1024, tn=1024):
    bsz, s, d = x.shape
    n = w.shape[1]
    half = RET_DK // 2
    invf = np.power(np.float32(RET_THETA),
                    -np.arange(half, dtype=np.float32) * np.float32(2.0) / np.float32(RET_DK))
    invf = jnp.asarray(invf.astype(np.float32).reshape(1, half))
    return pl.pallas_call(
        functools.partial(_ret_proj_kernel, tn=tn),
        grid=(bsz, s // tm, n // tn),
        in_specs=[
            pl.BlockSpec((None, tm, d), lambda b, i, j: (b, i, 0)),
            pl.BlockSpec((None, tm, 1), lambda b, i, j: (b, i, 0)),
            pl.BlockSpec((1, d), lambda b, i, j: (0, 0)),
            _mod_spec(layer, 0),
            _mod_spec(layer, 1),
            pl.BlockSpec((1, half), lambda b, i, j: (0, 0)),
            pl.BlockSpec((d, tn), lambda b, i, j: (0, j)),
        ],
        out_specs=pl.BlockSpec((None, tm, tn), lambda b, i, j: (b, i, j)),
        out_shape=jax.ShapeDtypeStruct((bsz, s, n), BF16),
        scratch_shapes=[pltpu.VMEM((tm, d), BF16), pltpu.VMEM((tm, half), F32),
                        pltpu.VMEM((tm, half), F32)],
        compiler_params=_params("parallel", "parallel", "arbitrary"),
        name="ret_proj",
    )(x, pos3, gain, mod, mod, invf, w)


def _ret_tables(chunk):
    lg = np.log1p(-np.exp2(-5.0 - np.arange(RET_HEADS, dtype=np.float64)))
    idx = np.arange(chunk, dtype=np.float64)
    diff = idx[:, None] - idx[None, :]
    decay = np.where(diff[None] >= 0, np.exp(np.maximum(diff, 0.0)[None] * lg[:, None, None]), 0.0)
    xi = np.exp((idx + 1.0)[None, :] * lg[:, None])
    zeta = np.exp((chunk - 1.0 - idx)[None, :] * lg[:, None])
    cd = np.exp(chunk * lg)
    return (jnp.asarray(decay.astype(np.float32)),
            jnp.asarray(xi.astype(np.float32)[:, :, None]),
            jnp.asarray(zeta.astype(np.float32)[:, :, None]),
            [float(v) for v in cd])


def _ret_kernel(q_ref, k_ref, v_ref, g_ref, decay_ref, xi_ref, zeta_ref, o_ref, state_ref,
                *, tt, chunk, cd):
    @pl.when(pl.program_id(1) == 0)
    def _():
        state_ref[...] = jnp.zeros_like(state_ref)

    for h in range(RET_HEADS):
        qs = slice(h * RET_DK, (h + 1) * RET_DK)
        vs = slice(h * RET_DV, (h + 1) * RET_DV)
        for c in range(tt // chunk):
            rows = slice(c * chunk, (c + 1) * chunk)
            qc = q_ref[rows, qs]
            kc = k_ref[rows, qs]
            vc = v_ref[rows, vs]
            st = state_ref[h]
            scores = _dot_nt(qc, kc) * decay_ref[h]
            inner = _dot(scores.astype(BF16), vc)
            cross = _dot(qc, st.astype(BF16)) * xi_ref[h]
            o = inner + cross
            kz = (kc.astype(F32) * zeta_ref[h]).astype(BF16)
            state_ref[h] = cd[h] * st + _dot_tn(kz, vc)
            mu = jnp.mean(o, axis=-1, keepdims=True)
            oc = o - mu
            var = jnp.mean(oc * oc, axis=-1, keepdims=True)
            on = oc * lax.rsqrt(var + EPS)
            g = g_ref[rows, vs].astype(F32)
            o_ref[rows, vs] = (g * jax.nn.sigmoid(g) * on).astype(BF16)


def _retention(proj, tt=512, chunk=RET_CHUNK):
    bsz, s, _ = proj.shape
    decay, xi, zeta, cd = _ret_tables(chunk)
    const3 = lambda b, n: (0, 0, 0)
    return pl.pallas_call(
        functools.partial(_ret_kernel, tt=tt, chunk=chunk, cd=cd),
        grid=(bsz, s // tt),
        in_specs=[
            pl.BlockSpec((None, tt, RET_QK), lambda b, n: (b, n, 0)),
            pl.BlockSpec((None, tt, RET_QK), lambda b, n: (b, n, 1)),
            pl.BlockSpec((None, tt, RET_V), lambda b, n: (b, n, 1)),
            pl.BlockSpec((None, tt, RET_V), lambda b, n: (b, n, 2)),
            pl.BlockSpec((RET_HEADS, chunk, chunk), const3),
            pl.BlockSpec((RET_HEADS, chunk, 1), const3),
            pl.BlockSpec((RET_HEADS, chunk, 1), const3),
        ],
        out_specs=pl.BlockSpec((None, tt, RET_V), lambda b, n: (b, n, 0)),
        out_shape=jax.ShapeDtypeStruct((bsz, s, RET_V), BF16),
        scratch_shapes=[pltpu.VMEM((RET_HEADS, RET_DK, RET_DV), F32)],
        compiler_params=_params("parallel", "arbitrary"),
        name="retention",
    )(proj, proj, proj, proj, decay, xi, zeta)


def _out_proj_kernel(a_ref, w_ref, x_ref, gate_ref, o_ref):
    o_ref[...] = x_ref[...] + gate_ref[...] * _dot(a_ref[...], w_ref[...])


def _out_proj(a, w, x, mod, layer, tm=512, tn=1024):
    bsz, s, k = a.shape
    d = w.shape[1]
    return pl.pallas_call(
        _out_proj_kernel,
        grid=(d // tn, bsz, s // tm),
        in_specs=[
            pl.BlockSpec((None, tm, k), lambda j, b, i: (b, i, 0)),
            pl.BlockSpec((k, tn), lambda j, b, i: (0, j)),
            pl.BlockSpec((None, tm, tn), lambda j, b, i: (b, i, j)),
            pl.BlockSpec((None, None, None, 1, tn), lambda j, b, i: (layer, b, 2, 0, j)),
        ],
        out_specs=pl.BlockSpec((None, tm, tn), lambda j, b, i: (b, i, j)),
        out_shape=jax.ShapeDtypeStruct((bsz, s, d), F32),
        compiler_params=_params("parallel", "parallel", "parallel"),
        name="out_proj",
    )(a, w, x, mod)


def _perm_matrix(tm, dil):
    t = np.arange(tm)
    p = (t % dil) * (tm // dil) + t // dil
    m = np.zeros((tm, tm), np.float32)
    m[p, t] = 1.0
    return m


def _mlp_kernel(*refs, final_norm, emit_next, tm):
    if emit_next:
        (x_ref, gain_ref, shift_ref, scale_ref, gate_ref, w1_ref, w2_ref, fgain_ref,
         ngain_ref, nshift_ref, nscale_ref, p4_ref, p16_ref,
         o_ref, h1_ref, h4_ref, h16_ref, h_ref) = refs
    else:
        (x_ref, gain_ref, shift_ref, scale_ref, gate_ref, w1_ref, w2_ref, fgain_ref,
         o_ref, h_ref) = refs
    k = pl.program_id(2)

    @pl.when(k == 0)
    def _():
        h = _norm_modulate(x_ref[...], gain_ref[...], shift_ref[...], scale_ref[...])
        h_ref[...] = h.astype(BF16)
        o_ref[...] = jnp.zeros_like(o_ref)

    a = jnp.maximum(_dot(h_ref[...], w1_ref[...]), 0.0)
    o_ref[...] += _dot((a * a).astype(BF16), w2_ref[...])

    @pl.when(k == pl.num_programs(2) - 1)
    def _():
        y = x_ref[...] + gate_ref[...] * o_ref[...]
        if final_norm:
            ms = jnp.mean(y * y, axis=-1, keepdims=True)
            o_ref[...] = y * lax.rsqrt(ms + EPS) * fgain_ref[...]
        else:
            o_ref[...] = y
        if emit_next:
            d = y.shape[-1]
            hn = _norm_modulate(y, ngain_ref[...], nshift_ref[...], nscale_ref[...]).astype(BF16)
            h1_ref[...] = hn.reshape(1, tm, d)
            h4_ref[...] = _dot(p4_ref[...], hn).astype(BF16).reshape(4, tm // 4, d)
            h16_ref[...] = _dot(p16_ref[...], hn).astype(BF16).reshape(16, tm // 16, d)


def _mlp(x, mod, layer, gain, w1, w2, fgain, final_norm, next_gain=None, tm=512, tk=1024):
    bsz, s, d = x.shape
    dff = w1.shape[1]
    emit_next = next_gain is not None
    row = lambda b, i, k: (0, 0)
    in_specs = [
        pl.BlockSpec((None, tm, d), lambda b, i, k: (b, i, 0)),
        pl.BlockSpec((1, d), row),
        _mod_spec(layer, 3),
        _mod_spec(layer, 4),
        _mod_spec(layer, 5),
        pl.BlockSpec((d, tk), lambda b, i, k: (0, k)),
        pl.BlockSpec((tk, d), lambda b, i, k: (k, 0)),
        pl.BlockSpec((1, d), row),
    ]
    operands = [x, gain, mod, mod, mod, w1, w2, fgain]
    out_specs = [pl.BlockSpec((None, tm, d), lambda b, i, k: (b, i, 0))]
    out_shapes = [jax.ShapeDtypeStruct((bsz, s, d), F32)]
    if emit_next:
        in_specs += [pl.BlockSpec((1, d), row), _mod_spec(layer + 1, 0), _mod_spec(layer + 1, 1),
                     pl.BlockSpec((tm, tm), row), pl.BlockSpec((tm, tm), row)]
        operands += [next_gain, mod, mod,
                     jnp.asarray(_perm_matrix(tm, 4), BF16), jnp.asarray(_perm_matrix(tm, 16), BF16)]
        for dil in (1, 4, 16):
            out_specs.append(pl.BlockSpec((None, dil, tm // dil, d), lambda b, i, k: (b, 0, i, 0)))
            out_shapes.append(jax.ShapeDtypeStruct((bsz, dil, s // dil, d), BF16))
    if not emit_next:
        out_specs, out_shapes = out_specs[0], out_shapes[0]
    return pl.pallas_call(
        functools.partial(_mlp_kernel, final_norm=final_norm, emit_next=emit_next, tm=tm),
        grid=(bsz, s // tm, dff // tk),
        in_specs=in_specs,
        out_specs=out_specs,
        out_shape=out_shapes,
        scratch_shapes=[pltpu.VMEM((tm, d), BF16)],
        compiler_params=_params("parallel", "parallel", "arbitrary"),
        name="mlp",
    )(*operands)


def _att_proj_kernel(h_ref, pos_ref, invf_ref, w_ref, o_ref, cos_ref, sin_ref, *, tn):
    j = pl.program_id(2)
    half = ROT_DIM // 2

    @pl.when(j == 0)
    def _():
        ang = pos_ref[...].astype(F32) * invf_ref[...]
        lane = lax.broadcasted_iota(jnp.int32, ang.shape, 1)
        cos_ref[...] = jnp.cos(ang)
        sin_ref[...] = jnp.where(lane < half, -jnp.sin(ang), jnp.sin(ang))

    qscale = jnp.where(j == 0, ATT_HEAD_DIM ** -0.5, 1.0).astype(F32)
    c = jnp.where(j < 2, cos_ref[...] * qscale, 1.0)
    s = jnp.where(j < 2, sin_ref[...] * qscale, 0.0)
    first_half = lax.broadcasted_iota(jnp.int32, c.shape, 1) < half
    h = h_ref[...]
    cw = 2 * LANES
    for cb in range(tn // cw):
        acc = _dot(h, w_ref[:, cb * cw:(cb + 1) * cw])
        for hh in range(cw // LANES):
            blk = acc[:, hh * LANES:(hh + 1) * LANES]
            partner = jnp.where(first_half, pltpu.roll(blk, LANES - half, 1),
                                pltpu.roll(blk, half, 1))
            lo = cb * cw + hh * LANES
            o_ref[:, lo:lo + LANES] = (blk * c + partner * s).astype(BF16)


def _att_proj(h, pos, w, group, tm=1024, tn=ATT_WIDTH):
    bsz, s, d = h.shape
    half = ROT_DIM // 2
    invf = np.power(np.float32(ROPE_THETA),
                    -np.arange(half, dtype=np.float32) * np.float32(2.0) / np.float32(ROT_DIM))
    invf_full = np.zeros((1, LANES), np.float32)
    invf_full[0, :half] = invf
    invf_full[0, half:ROT_DIM] = invf
    return pl.pallas_call(
        functools.partial(_att_proj_kernel, tn=tn),
        grid=(bsz, s // tm, 3),
        in_specs=[
            pl.BlockSpec((None, tm, d), lambda b, i, j: (b, i, 0)),
            pl.BlockSpec((None, tm, 1), lambda b, i, j: (b, i, 0)),
            pl.BlockSpec((1, LANES), lambda b, i, j: (0, 0)),
            pl.BlockSpec((d, tn), lambda b, i, j: (0, 3 * group + j)),
        ],
        out_specs=pl.BlockSpec((None, tm, tn), lambda b, i, j: (b, i, j)),
        out_shape=jax.ShapeDtypeStruct((bsz, s, 3 * tn), BF16),
        scratch_shapes=[pltpu.VMEM((tm, LANES), F32), pltpu.VMEM((tm, LANES), F32)],
        compiler_params=_params("parallel", "parallel", "arbitrary"),
        name=f"att_proj_g{group}",
    )(h, pos, jnp.asarray(invf_full), w)


def _att_kernel(q_ref, kc_ref, kp_ref, vc_ref, vp_ref, o_ref, st_ref, *, tq):
    n = pl.program_id(2)
    blk = ATT_BLOCK
    row = lax.broadcasted_iota(jnp.int32, (blk, 2 * blk), 0)
    col = lax.broadcasted_iota(jnp.int32, (blk, 2 * blk), 1)
    band = (col >= row) & (col <= row + blk)
    band_first = band & ((col >= blk) | (n > 0))
    lane = lax.broadcasted_iota(jnp.int32, (blk, LANES), 1)
    for i in range(tq // blk):
        rows = slice(i * blk, (i + 1) * blk)
        stats = jnp.zeros((blk, LANES), F32)
        for h in range(ATT_HEADS):
            hs = slice(h * ATT_HEAD_DIM, (h + 1) * ATT_HEAD_DIM)
            qi = q_ref[rows, hs]
            if i == 0:
                kk = jnp.concatenate([kp_ref[:, hs], kc_ref[0:blk, hs]], axis=0)
                vv = jnp.concatenate([vp_ref[:, hs], vc_ref[0:blk, hs]], axis=0)
                mask = band_first
            else:
                kk = kc_ref[(i - 1) * blk:(i + 1) * blk, hs]
                vv = vc_ref[(i - 1) * blk:(i + 1) * blk, hs]
                mask = band
            s = jnp.where(mask, _dot_nt(qi, kk), -jnp.inf)
            m = jnp.max(s, axis=-1, keepdims=True)
            p = jnp.exp(s - m)
            l = jnp.sum(p, axis=-1, keepdims=True)
            o = _dot(p.astype(BF16), vv) / l
            o_ref[rows, hs] = o.astype(BF16)
            stats = jnp.where(lane == h, m + jnp.log(l), stats)
        hi = stats.astype(BF16)
        r1 = stats - hi.astype(F32)
        mid = r1.astype(BF16)
        lo = (r1 - mid.astype(F32)).astype(BF16)
        st_ref[rows, 0:LANES] = hi
        st_ref[rows, LANES:2 * LANES] = mid
        st_ref[rows, 2 * LANES:3 * LANES] = lo


def _attention(qkv, tq=256):
    bsz, dil, ln, _ = qkv.shape
    tq = min(tq, ln)
    per = tq // ATT_BLOCK
    cur = lambda c: pl.BlockSpec((None, None, tq, ATT_WIDTH), lambda b, r, n: (b, r, n, c))
    prev = lambda c: pl.BlockSpec((None, None, ATT_BLOCK, ATT_WIDTH),
                                  lambda b, r, n: (b, r, jnp.maximum(n * per - 1, 0), c))
    return pl.pallas_call(
        functools.partial(_att_kernel, tq=tq),
        grid=(bsz, dil, ln // tq),
        in_specs=[cur(0), cur(1), prev(1), cur(2), prev(2)],
        out_specs=[pl.BlockSpec((None, None, tq, ATT_WIDTH), lambda b, r, n: (b, r, n, 0)),
                   pl.BlockSpec((None, None, tq, 3 * LANES), lambda b, r, n: (b, r, n, 0))],
        out_shape=[jax.ShapeDtypeStruct((bsz, dil, ln, ATT_WIDTH), BF16),
                   jax.ShapeDtypeStruct((bsz, dil, ln, 3 * LANES), BF16)],
        compiler_params=_params("parallel", "parallel", "parallel"),
        name=f"attention_d{dil}",
    )(qkv, qkv, qkv, qkv, qkv)


def _att_merge_kernel(o1_ref, o4_ref, o16_ref, s1_ref, s4_ref, s16_ref, pt4_ref, pt16_ref,
                      ex_ref, w_ref, x_ref, gate_ref, out_ref, lhs_ref, *, tm):
    def lse_of(pieces):
        return pieces[:, 0:LANES] + pieces[:, LANES:2 * LANES] + pieces[:, 2 * LANES:3 * LANES]

    lse1 = lse_of(s1_ref[...].astype(F32))
    lse4 = lse_of(_dot(pt4_ref[...], s4_ref[...].reshape(tm, 3 * LANES)))
    lse16 = lse_of(_dot(pt16_ref[...], s16_ref[...].reshape(tm, 3 * LANES)))
    m = jnp.maximum(jnp.maximum(lse1, lse4), lse16)
    e1 = jnp.exp(lse1 - m)
    e4 = jnp.exp(lse4 - m)
    e16 = jnp.exp(lse16 - m)
    inv = 1.0 / (e1 + e4 + e16)

    def split(w):
        hi = w.astype(BF16)
        lo = (w - hi.astype(F32)).astype(BF16)
        return jnp.concatenate([hi, lo], axis=1)

    w1, w4, w16 = split(e1 * inv), split(e4 * inv), split(e16 * inv)
    cw = 4 * ATT_HEAD_DIM
    for cb in range(ATT_WIDTH // cw):
        cs = slice(cb * cw, (cb + 1) * cw)
        ex = ex_ref[:, cs]
        a1 = o1_ref[:, cs].astype(F32)
        a4 = _dot(pt4_ref[...], o4_ref[:, :, cs].reshape(tm, cw))
        a16 = _dot(pt16_ref[...], o16_ref[:, :, cs].reshape(tm, cw))
        mix = _dot(w1, ex) * a1 + _dot(w4, ex) * a4 + _dot(w16, ex) * a16
        lhs_ref[:, cs] = mix.astype(BF16)
    out_ref[...] = x_ref[...] + gate_ref[...] * _dot(lhs_ref[...], w_ref[...])


def _att_merge(outs, stats, w, x, mod, layer, tm=256):
    bsz, s, d = x.shape
    in_specs, operands = [], []
    for arr in list(outs) + list(stats):
        dil, width = arr.shape[1], arr.shape[3]
        if dil == 1:
            in_specs.append(pl.BlockSpec((None, None, tm, width), lambda b, i: (b, 0, i, 0)))
        else:
            in_specs.append(pl.BlockSpec((None, dil, tm // dil, width), lambda b, i: (b, 0, i, 0)))
        operands.append(arr)
    expand = np.zeros((2 * LANES, ATT_WIDTH), np.float32)
    for h in range(ATT_HEADS):
        expand[h, h * ATT_HEAD_DIM:(h + 1) * ATT_HEAD_DIM] = 1.0
        expand[LANES + h, h * ATT_HEAD_DIM:(h + 1) * ATT_HEAD_DIM] = 1.0
    const2 = lambda b, i: (0, 0)
    in_specs += [
        pl.BlockSpec((tm, tm), const2),
        pl.BlockSpec((tm, tm), const2),
        pl.BlockSpec((2 * LANES, ATT_WIDTH), const2),
        pl.BlockSpec((ATT_WIDTH, d), const2),
        pl.BlockSpec((None, tm, d), lambda b, i: (b, i, 0)),
        _mod_spec(layer, 2),
    ]
    operands += [jnp.asarray(_perm_matrix(tm, 4).T, BF16), jnp.asarray(_perm_matrix(tm, 16).T, BF16),
                 jnp.asarray(expand, BF16), w, x, mod]
    return pl.pallas_call(
        functools.partial(_att_merge_kernel, tm=tm),
        grid=(bsz, s // tm),
        in_specs=in_specs,
        out_specs=pl.BlockSpec((None, tm, d), lambda b, i: (b, i, 0)),
        out_shape=jax.ShapeDtypeStruct((bsz, s, d), F32),
        scratch_shapes=[pltpu.VMEM((tm, ATT_WIDTH), BF16)],
        compiler_params=_params("parallel", "parallel"),
        name="att_merge",
    )(*operands)


def kernel(x, c, positions, ada_w, ada_b, norm_mix, norm_mlp, ret_w_in, ret_w_out,
           att_w_in, att_w_out, mlp_w1, mlp_w2, final_norm):
    bsz, s, d = x.shape
    depth = ada_w.shape[0]
    assert d == D_MODEL and depth == 2 and s % 2048 == 0
    mod = _ada(c, ada_w, ada_b).reshape(depth, bsz, 6, 1, d)
    pos3 = positions.reshape(bsz, s, 1)
    fgain = final_norm.reshape(1, d)

    proj = _ret_proj(x, pos3, mod, 0, norm_mix[0].reshape(1, d), ret_w_in[0].astype(BF16))
    ret = _retention(proj)
    x = _out_proj(ret, ret_w_out[0].astype(BF16), x, mod, 0)
    x, *hs = _mlp(x, mod, 0, norm_mlp[0].reshape(1, d), mlp_w1[0].astype(BF16), mlp_w2[0].astype(BF16),
                  fgain, final_norm=False, next_gain=norm_mix[1].reshape(1, d))

    w_att = att_w_in[0].astype(BF16)
    outs, stats = [], []
    for gi, (_, dil) in enumerate(ATT_GROUPS):
        pos_g = positions.reshape(bsz, s // dil, dil).transpose(0, 2, 1).reshape(bsz, s, 1)
        qkv = _att_proj(hs[gi].reshape(bsz, s, d), pos_g, w_att, gi)
        o_g, st_g = _attention(qkv.reshape(bsz, dil, s // dil, 3 * ATT_WIDTH))
        outs.append(o_g)
        stats.append(st_g)
    x = _att_merge(outs, stats, att_w_out[0].astype(BF16), x, mod, 1)
    x = _mlp(x, mod, 1, norm_mlp[1].reshape(1, d), mlp_w1[1].astype(BF16), mlp_w2[1].astype(BF16),
             fgain, final_norm=True)
    return x
```

```python
import functools

import numpy as np
import jax
import jax.numpy as jnp
from jax import lax
from jax.experimental import pallas as pl
from jax.experimental.pallas import tpu as pltpu

F32 = jnp.float32
BF16 = jnp.bfloat16

EPS = 1e-6
D_MODEL = 2048
D_FF = 4 * D_MODEL
RET_HEADS = 8
RET_DK = D_MODEL // RET_HEADS
RET_DV = 2 * RET_DK
RET_QK = RET_HEADS * RET_DK
RET_V = RET_HEADS * RET_DV
RET_PROJ = 2 * RET_QK + 2 * RET_V
RET_THETA = 10000.0
RET_CHUNK = 256
ATT_HEADS = 16
ATT_HEAD_DIM = D_MODEL // ATT_HEADS
ATT_WIDTH = ATT_HEADS * ATT_HEAD_DIM
ATT_GROUPS = ((128, 1), (512, 4), (2048, 16))
ATT_BLOCK = 128
ROT_DIM = ATT_HEAD_DIM // 4
ROPE_THETA = 500000.0
LANES = 128

VMEM_LIMIT_BYTES = 56 * 1024 * 1024


def _params(*sem):
    return pltpu.CompilerParams(dimension_semantics=sem, vmem_limit_bytes=VMEM_LIMIT_BYTES)


def _dot(a, b):
    return jnp.dot(a, b, preferred_element_type=F32)


def _dot_nt(a, b):
    return lax.dot_general(a, b, (((1,), (1,)), ((), ())), preferred_element_type=F32)


def _dot_tn(a, b):
    return lax.dot_general(a, b, (((0,), (0,)), ((), ())), preferred_element_type=F32)


def _norm_modulate(x, gain, shift, scale):
    ms = jnp.mean(x * x, axis=-1, keepdims=True)
    y = x * lax.rsqrt(ms + EPS) * gain
    return y * (1.0 + scale) + shift


def _ada_kernel(ct_ref, w_ref, b_ref, o_ref):
    ct = ct_ref[...]
    cs = ct * jax.nn.sigmoid(ct)
    w = w_ref[...]
    for b in range(o_ref.shape[0]):
        o_ref[b:b + 1, :] = jnp.sum(w * cs[:, b:b + 1], axis=0, keepdims=True) + b_ref[...]


def _ada(c, ada_w, ada_b, tn=1024):
    depth, d, n = ada_w.shape
    bsz = c.shape[0]
    return pl.pallas_call(
        _ada_kernel,
        grid=(depth, n // tn),
        in_specs=[
            pl.BlockSpec((d, bsz), lambda l, j: (0, 0)),
            pl.BlockSpec((None, d, tn), lambda l, j: (l, 0, j)),
            pl.BlockSpec((None, 1, tn), lambda l, j: (l, 0, j)),
        ],
        out_specs=pl.BlockSpec((None, bsz, tn), lambda l, j: (l, 0, j)),
        out_shape=jax.ShapeDtypeStruct((depth, bsz, n), F32),
        compiler_params=_params("parallel", "parallel"),
        name="ada_mod",
    )(c.T, ada_w, ada_b.reshape(depth, 1, n))


def _mod_spec(layer, which):
    return pl.BlockSpec((None, None, None, 1, D_MODEL),
                        lambda b, *_: (layer, b, which, 0, 0))


def _ret_proj_kernel(x_ref, pos_ref, gain_ref, shift_ref, scale_ref, invf_ref, w_ref,
                     o_ref, h_ref, cos_ref, sin_ref, *, tn):
    j = pl.program_id(2)

    @pl.when(j == 0)
    def _():
        h = _norm_modulate(x_ref[...], gain_ref[...], shift_ref[...], scale_ref[...])
        h_ref[...] = h.astype(BF16)
        ang = pos_ref[...].astype(F32) * invf_ref[...]
        cos_ref[...] = jnp.cos(ang)
        sin_ref[...] = jnp.sin(ang)

    n_qk = 2 * RET_QK // tn
    kscale = jnp.where(j >= n_qk // 2, RET_DK ** -0.5, 1.0).astype(F32)
    rotated = j < n_qk
    c = jnp.where(rotated, cos_ref[...] * kscale, 1.0)
    s = jnp.where(rotated, sin_ref[...] * kscale, 0.0)
    half = RET_DK // 2
    h = h_ref[...]
    for hh in range(tn // RET_DK):
        lo = hh * RET_DK
        acc = _dot(h, w_ref[:, lo:lo + RET_DK])
        a1 = acc[:, :half]
        a2 = acc[:, half:]
        o_ref[:, lo:lo + half] = (a1 * c - a2 * s).astype(BF16)
        o_ref[:, lo + half:lo + RET_DK] = (a2 * c + a1 * s).astype(BF16)


def _ret_proj(x, pos3, mod, layer, gain, w, tm=1024, tn=2048):
    bsz, s, d = x.shape
    n = w.shape[1]
    half = RET_DK // 2
    invf = np.power(np.float32(RET_THETA),
                    -np.arange(half, dtype=np.float32) * np.float32(2.0) / np.float32(RET_DK))
    invf = jnp.asarray(invf.astype(np.float32).reshape(1, half))
    return pl.pallas_call(
        functools.partial(_ret_proj_kernel, tn=tn),
        grid=(bsz, s // tm, n // tn),
        in_specs=[
            pl.BlockSpec((None, tm, d), lambda b, i, j: (b, i, 0)),
            pl.BlockSpec((None, tm, 1), lambda b, i, j: (b, i, 0)),
            pl.BlockSpec((1, d), lambda b, i, j: (0, 0)),
            _mod_spec(layer, 0),
            _mod_spec(layer, 1),
            pl.BlockSpec((1, half), lambda b, i, j: (0, 0)),
            pl.BlockSpec((d, tn), lambda b, i, j: (0, j)),
        ],
        out_specs=pl.BlockSpec((None, tm, tn), lambda b, i, j: (b, i, j)),
        out_shape=jax.ShapeDtypeStruct((bsz, s, n), BF16),
        scratch_shapes=[pltpu.VMEM((tm, d), BF16), pltpu.VMEM((tm, half), F32),
                        pltpu.VMEM((tm, half), F32)],
        compiler_params=_params("parallel", "parallel", "arbitrary"),
        name="ret_proj",
    )(x, pos3, gain, mod, mod, invf, w)


def _ret_tables(chunk):
    lg = np.log1p(-np.exp2(-5.0 - np.arange(RET_HEADS, dtype=np.float64)))
    idx = np.arange(chunk, dtype=np.float64)
    diff = idx[:, None] - idx[None, :]
    decay = np.where(diff[None] >= 0, np.exp(np.maximum(diff, 0.0)[None] * lg[:, None, None]), 0.0)
    xi = np.exp((idx + 1.0)[None, :] * lg[:, None])
    zeta = np.exp((chunk - 1.0 - idx)[None, :] * lg[:, None])
    cd = np.exp(chunk * lg)
    return (jnp.asarray(decay.astype(np.float32)),
            jnp.asarray(xi.astype(np.float32)[:, :, None]),
            jnp.asarray(zeta.astype(np.float32)[:, :, None]),
            [float(v) for v in cd])


def _ret_kernel(q_ref, k_ref, v_ref, g_ref, decay_ref, xi_ref, zeta_ref, o_ref, state_ref,
                *, tt, chunk, cd):
    @pl.when(pl.program_id(1) == 0)
    def _():
        state_ref[...] = jnp.zeros_like(state_ref)

    for h in range(RET_HEADS):
        qs = slice(h * RET_DK, (h + 1) * RET_DK)
        vs = slice(h * RET_DV, (h + 1) * RET_DV)
        for c in range(tt // chunk):
            rows = slice(c * chunk, (c + 1) * chunk)
            qc = q_ref[rows, qs]
            kc = k_ref[rows, qs]
            vc = v_ref[rows, vs]
            st = state_ref[h]
            scores = _dot_nt(qc, kc) * decay_ref[h]
            inner = _dot(scores.astype(BF16), vc)
            cross = _dot(qc, st.astype(BF16)) * xi_ref[h]
            o = inner + cross
            kz = (kc.astype(F32) * zeta_ref[h]).astype(BF16)
            state_ref[h] = cd[h] * st + _dot_tn(kz, vc)
            mu = jnp.mean(o, axis=-1, keepdims=True)
            oc = o - mu
            var = jnp.mean(oc * oc, axis=-1, keepdims=True)
            on = oc * lax.rsqrt(var + EPS)
            g = g_ref[rows, vs].astype(F32)
            o_ref[rows, vs] = (g * jax.nn.sigmoid(g) * on).astype(BF16)


def _retention(proj, tt=512, chunk=RET_CHUNK):
    bsz, s, _ = proj.shape
    decay, xi, zeta, cd = _ret_tables(chunk)
    const3 = lambda b, n: (0, 0, 0)
    return pl.pallas_call(
        functools.partial(_ret_kernel, tt=tt, chunk=chunk, cd=cd),
        grid=(bsz, s // tt),
        in_specs=[
            pl.BlockSpec((None, tt, RET_QK), lambda b, n: (b, n, 0)),
            pl.BlockSpec((None, tt, RET_QK), lambda b, n: (b, n, 1)),
            pl.BlockSpec((None, tt, RET_V), lambda b, n: (b, n, 1)),
            pl.BlockSpec((None, tt, RET_V), lambda b, n: (b, n, 2)),
            pl.BlockSpec((RET_HEADS, chunk, chunk), const3),
            pl.BlockSpec((RET_HEADS, chunk, 1), const3),
            pl.BlockSpec((RET_HEADS, chunk, 1), const3),
        ],
        out_specs=pl.BlockSpec((None, tt, RET_V), lambda b, n: (b, n, 0)),
        out_shape=jax.ShapeDtypeStruct((bsz, s, RET_V), BF16),
        scratch_shapes=[pltpu.VMEM((RET_HEADS, RET_DK, RET_DV), F32)],
        compiler_params=_params("parallel", "arbitrary"),
        name="retention",
    )(proj, proj, proj, proj, decay, xi, zeta)


def _out_proj_kernel(a_ref, w_ref, x_ref, gate_ref, o_ref):
    o_ref[...] = x_ref[...] + gate_ref[...] * _dot(a_ref[...], w_ref[...])


def _out_proj(a, w, x, mod, layer, tm=512, tn=1024):
    bsz, s, k = a.shape
    d = w.shape[1]
    return pl.pallas_call(
        _out_proj_kernel,
        grid=(d // tn, bsz, s // tm),
        in_specs=[
            pl.BlockSpec((None, tm, k), lambda j, b, i: (b, i, 0)),
            pl.BlockSpec((k, tn), lambda j, b, i: (0, j)),
            pl.BlockSpec((None, tm, tn), lambda j, b, i: (b, i, j)),
            pl.BlockSpec((None, None, None, 1, tn), lambda j, b, i: (layer, b, 2, 0, j)),
        ],
        out_specs=pl.BlockSpec((None, tm, tn), lambda j, b, i: (b, i, j)),
        out_shape=jax.ShapeDtypeStruct((bsz, s, d), F32),
        compiler_params=_params("parallel", "parallel", "parallel"),
        name="out_proj",
    )(a, w, x, mod)


def _perm_matrix(tm, dil):
    t = np.arange(tm)
    p = (t % dil) * (tm // dil) + t // dil
    m = np.zeros((tm, tm), np.float32)
    m[p, t] = 1.0
    return m


def _mlp_kernel(*refs, final_norm, emit_next, tm):
    if emit_next:
        (x_ref, gain_ref, shift_ref, scale_ref, gate_ref, w1_ref, w2_ref, fgain_ref,
         ngain_ref, nshift_ref, nscale_ref, p4_ref, p16_ref,
         o_ref, h1_ref, h4_ref, h16_ref, h_ref) = refs
    else:
        (x_ref, gain_ref, shift_ref, scale_ref, gate_ref, w1_ref, w2_ref, fgain_ref,
         o_ref, h_ref) = refs
    k = pl.program_id(2)

    @pl.when(k == 0)
    def _():
        h = _norm_modulate(x_ref[...], gain_ref[...], shift_ref[...], scale_ref[...])
        h_ref[...] = h.astype(BF16)
        o_ref[...] = jnp.zeros_like(o_ref)

    a = jnp.maximum(_dot(h_ref[...], w1_ref[...]), 0.0)
    o_ref[...] += _dot((a * a).astype(BF16), w2_ref[...])

    @pl.when(k == pl.num_programs(2) - 1)
    def _():
        y = x_ref[...] + gate_ref[...] * o_ref[...]
        if final_norm:
            ms = jnp.mean(y * y, axis=-1, keepdims=True)
            o_ref[...] = y * lax.rsqrt(ms + EPS) * fgain_ref[...]
        else:
            o_ref[...] = y
        if emit_next:
            d = y.shape[-1]
            hn = _norm_modulate(y, ngain_ref[...], nshift_ref[...], nscale_ref[...]).astype(BF16)
            h1_ref[...] = hn.reshape(1, tm, d)
            h4_ref[...] = _dot(p4_ref[...], hn).astype(BF16).reshape(4, tm // 4, d)
            h16_ref[...] = _dot(p16_ref[...], hn).astype(BF16).reshape(16, tm // 16, d)


def _mlp(x, mod, layer, gain, w1, w2, fgain, final_norm, next_gain=None, tm=512, tk=1024):
    bsz, s, d = x.shape
    dff = w1.shape[2]
    emit_next = next_gain is not None
    row = lambda b, i, k: (0, 0)
    in_specs = [
        pl.BlockSpec((None, tm, d), lambda b, i, k: (b, i, 0)),
        pl.BlockSpec((1, d), row),
        _mod_spec(layer, 3),
        _mod_spec(layer, 4),
        _mod_spec(layer, 5),
        pl.BlockSpec((None, d, tk), lambda b, i, k: (layer, 0, k)),
        pl.BlockSpec((None, tk, d), lambda b, i, k: (layer, k, 0)),
        pl.BlockSpec((1, d), row),
    ]
    operands = [x, gain, mod, mod, mod, w1, w2, fgain]
    out_specs = [pl.BlockSpec((None, tm, d), lambda b, i, k: (b, i, 0))]
    out_shapes = [jax.ShapeDtypeStruct((bsz, s, d), F32)]
    if emit_next:
        in_specs += [pl.BlockSpec((1, d), row), _mod_spec(layer + 1, 0), _mod_spec(layer + 1, 1),
                     pl.BlockSpec((tm, tm), row), pl.BlockSpec((tm, tm), row)]
        operands += [next_gain, mod, mod,
                     jnp.asarray(_perm_matrix(tm, 4), BF16), jnp.asarray(_perm_matrix(tm, 16), BF16)]
        for dil in (1, 4, 16):
            out_specs.append(pl.BlockSpec((None, dil, tm // dil, d), lambda b, i, k: (b, 0, i, 0)))
            out_shapes.append(jax.ShapeDtypeStruct((bsz, dil, s // dil, d), BF16))
    if not emit_next:
        out_specs, out_shapes = out_specs[0], out_shapes[0]
    return pl.pallas_call(
        functools.partial(_mlp_kernel, final_norm=final_norm, emit_next=emit_next, tm=tm),
        grid=(bsz, s // tm, dff // tk),
        in_specs=in_specs,
        out_specs=out_specs,
        out_shape=out_shapes,
        scratch_shapes=[pltpu.VMEM((tm, d), BF16)],
        compiler_params=_params("parallel", "parallel", "arbitrary"),
        name="mlp",
    )(*operands)


def _att_proj_kernel(h_ref, pos_ref, invf_ref, w_ref, o_ref):
    half = ROT_DIM // 2
    ang = pos_ref[...].astype(F32) * invf_ref[...]
    first_half = lax.broadcasted_iota(jnp.int32, ang.shape, 1) < half
    cos = jnp.cos(ang)
    sin = jnp.where(first_half, -jnp.sin(ang), jnp.sin(ang))
    qscale = ATT_HEAD_DIM ** -0.5
    rot = ((cos * qscale, sin * qscale), (cos, sin))
    h = h_ref[...]
    cw = 2 * LANES
    for cb in range(3 * ATT_WIDTH // cw):
        kind = cb * cw // ATT_WIDTH
        acc = _dot(h, w_ref[:, cb * cw:(cb + 1) * cw])
        for hh in range(cw // LANES):
            blk = acc[:, hh * LANES:(hh + 1) * LANES]
            if kind < 2:
                c, s = rot[kind]
                partner = jnp.where(first_half, pltpu.roll(blk, LANES - half, 1),
                                    pltpu.roll(blk, half, 1))
                blk = blk * c + partner * s
            lo = cb * cw + hh * LANES
            o_ref[:, lo:lo + LANES] = blk.astype(BF16)


def _att_proj(h, pos, w, group, tm=512):
    bsz, s, d = h.shape
    tn = 3 * ATT_WIDTH
    half = ROT_DIM // 2
    invf = np.power(np.float32(ROPE_THETA),
                    -np.arange(half, dtype=np.float32) * np.float32(2.0) / np.float32(ROT_DIM))
    invf_full = np.zeros((1, LANES), np.float32)
    invf_full[0, :half] = invf
    invf_full[0, half:ROT_DIM] = invf
    return pl.pallas_call(
        _att_proj_kernel,
        grid=(bsz, s // tm),
        in_specs=[
            pl.BlockSpec((None, tm, d), lambda b, i: (b, i, 0)),
            pl.BlockSpec((None, tm, 1), lambda b, i: (b, i, 0)),
            pl.BlockSpec((1, LANES), lambda b, i: (0, 0)),
            pl.BlockSpec((d, tn), lambda b, i: (0, group)),
        ],
        out_specs=pl.BlockSpec((None, tm, tn), lambda b, i: (b, i, 0)),
        out_shape=jax.ShapeDtypeStruct((bsz, s, tn), BF16),
        compiler_params=_params("parallel", "parallel"),
        name=f"att_proj_g{group}",
    )(h, pos, jnp.asarray(invf_full), w)


def _att_kernel(q_ref, kc_ref, kp_ref, vc_ref, vp_ref, o_ref, st_ref, *, tq):
    n = pl.program_id(2)
    blk = ATT_BLOCK
    row = lax.broadcasted_iota(jnp.int32, (blk, 2 * blk), 0)
    col = lax.broadcasted_iota(jnp.int32, (blk, 2 * blk), 1)
    band = (col >= row) & (col <= row + blk)
    band_first = band & ((col >= blk) | (n > 0))
    lane = lax.broadcasted_iota(jnp.int32, (blk, LANES), 1)
    for i in range(tq // blk):
        rows = slice(i * blk, (i + 1) * blk)
        stats = jnp.zeros((blk, LANES), F32)
        for h in range(ATT_HEADS):
            hs = slice(h * ATT_HEAD_DIM, (h + 1) * ATT_HEAD_DIM)
            qi = q_ref[rows, hs]
            if i == 0:
                kk = jnp.concatenate([kp_ref[:, hs], kc_ref[0:blk, hs]], axis=0)
                vv = jnp.concatenate([vp_ref[:, hs], vc_ref[0:blk, hs]], axis=0)
                mask = band_first
            else:
                kk = kc_ref[(i - 1) * blk:(i + 1) * blk, hs]
                vv = vc_ref[(i - 1) * blk:(i + 1) * blk, hs]
                mask = band
            s = jnp.where(mask, _dot_nt(qi, kk), -jnp.inf)
            m = jnp.max(s, axis=-1, keepdims=True)
            p = jnp.exp(s - m)
            l = jnp.sum(p, axis=-1, keepdims=True)
            o = _dot(p.astype(BF16), vv) / l
            o_ref[rows, hs] = o.astype(BF16)
            stats = jnp.where(lane == h, m + jnp.log(l), stats)
        hi = stats.astype(BF16)
        r1 = stats - hi.astype(F32)
        mid = r1.astype(BF16)
        lo = (r1 - mid.astype(F32)).astype(BF16)
        st_ref[rows, 0:LANES] = hi
        st_ref[rows, LANES:2 * LANES] = mid
        st_ref[rows, 2 * LANES:3 * LANES] = lo


def _attention(qkv, tq=256):
    bsz, dil, ln, _ = qkv.shape
    tq = min(tq, ln)
    per = tq // ATT_BLOCK
    cur = lambda c: pl.BlockSpec((None, None, tq, ATT_WIDTH), lambda b, r, n: (b, r, n, c))
    prev = lambda c: pl.BlockSpec((None, None, ATT_BLOCK, ATT_WIDTH),
                                  lambda b, r, n: (b, r, jnp.maximum(n * per - 1, 0), c))
    return pl.pallas_call(
        functools.partial(_att_kernel, tq=tq),
        grid=(bsz, dil, ln // tq),
        in_specs=[cur(0), cur(1), prev(1), cur(2), prev(2)],
        out_specs=[pl.BlockSpec((None, None, tq, ATT_WIDTH), lambda b, r, n: (b, r, n, 0)),
                   pl.BlockSpec((None, None, tq, 3 * LANES), lambda b, r, n: (b, r, n, 0))],
        out_shape=[jax.ShapeDtypeStruct((bsz, dil, ln, ATT_WIDTH), BF16),
                   jax.ShapeDtypeStruct((bsz, dil, ln, 3 * LANES), BF16)],
        compiler_params=_params("parallel", "parallel", "parallel"),
        name=f"attention_d{dil}",
    )(qkv, qkv, qkv, qkv, qkv)


def _att_merge_kernel(o1_ref, o4_ref, o16_ref, s1_ref, s4_ref, s16_ref, pt4_ref, pt16_ref,
                      ex_ref, w_ref, x_ref, gate_ref, out_ref, lhs_ref, *, tm):
    def lse_of(pieces):
        return pieces[:, 0:LANES] + pieces[:, LANES:2 * LANES] + pieces[:, 2 * LANES:3 * LANES]

    lse1 = lse_of(s1_ref[...].astype(F32))
    lse4 = lse_of(_dot(pt4_ref[...], s4_ref[...].reshape(tm, 3 * LANES)))
    lse16 = lse_of(_dot(pt16_ref[...], s16_ref[...].reshape(tm, 3 * LANES)))
    m = jnp.maximum(jnp.maximum(lse1, lse4), lse16)
    e1 = jnp.exp(lse1 - m)
    e4 = jnp.exp(lse4 - m)
    e16 = jnp.exp(lse16 - m)
    inv = 1.0 / (e1 + e4 + e16)

    def split(w):
        hi = w.astype(BF16)
        lo = (w - hi.astype(F32)).astype(BF16)
        return jnp.concatenate([hi, lo], axis=1)

    w1, w4, w16 = split(e1 * inv), split(e4 * inv), split(e16 * inv)
    cw = 4 * ATT_HEAD_DIM
    for cb in range(ATT_WIDTH // cw):
        cs = slice(cb * cw, (cb + 1) * cw)
        ex = ex_ref[:, cs]
        a1 = o1_ref[:, cs].astype(F32)
        a4 = _dot(pt4_ref[...], o4_ref[:, :, cs].reshape(tm, cw))
        a16 = _dot(pt16_ref[...], o16_ref[:, :, cs].reshape(tm, cw))
        mix = _dot(w1, ex) * a1 + _dot(w4, ex) * a4 + _dot(w16, ex) * a16
        lhs_ref[:, cs] = mix.astype(BF16)
    out_ref[...] = x_ref[...] + gate_ref[...] * _dot(lhs_ref[...], w_ref[...])


def _att_merge(outs, stats, w, x, mod, layer, tm=256):
    bsz, s, d = x.shape
    in_specs, operands = [], []
    for arr in list(outs) + list(stats):
        dil, width = arr.shape[1], arr.shape[3]
        if dil == 1:
            in_specs.append(pl.BlockSpec((None, None, tm, width), lambda b, i: (b, 0, i, 0)))
        else:
            in_specs.append(pl.BlockSpec((None, dil, tm // dil, width), lambda b, i: (b, 0, i, 0)))
        operands.append(arr)
    expand = np.zeros((2 * LANES, ATT_WIDTH), np.float32)
    for h in range(ATT_HEADS):
        expand[h, h * ATT_HEAD_DIM:(h + 1) * ATT_HEAD_DIM] = 1.0
        expand[LANES + h, h * ATT_HEAD_DIM:(h + 1) * ATT_HEAD_DIM] = 1.0
    const2 = lambda b, i: (0, 0)
    in_specs += [
        pl.BlockSpec((tm, tm), const2),
        pl.BlockSpec((tm, tm), const2),
        pl.BlockSpec((2 * LANES, ATT_WIDTH), const2),
        pl.BlockSpec((ATT_WIDTH, d), const2),
        pl.BlockSpec((None, tm, d), lambda b, i: (b, i, 0)),
        _mod_spec(layer, 2),
    ]
    operands += [jnp.asarray(_perm_matrix(tm, 4).T, BF16), jnp.asarray(_perm_matrix(tm, 16).T, BF16),
                 jnp.asarray(expand, BF16), w, x, mod]
    return pl.pallas_call(
        functools.partial(_att_merge_kernel, tm=tm),
        grid=(bsz, s // tm),
        in_specs=in_specs,
        out_specs=pl.BlockSpec((None, tm, d), lambda b, i: (b, i, 0)),
        out_shape=jax.ShapeDtypeStruct((bsz, s, d), F32),
        scratch_shapes=[pltpu.VMEM((tm, ATT_WIDTH), BF16)],
        compiler_params=_params("parallel", "parallel"),
        name="att_merge",
    )(*operands)


def kernel(x, c, positions, ada_w, ada_b, norm_mix, norm_mlp, ret_w_in, ret_w_out,
           att_w_in, att_w_out, mlp_w1, mlp_w2, final_norm):
    bsz, s, d = x.shape
    depth = ada_w.shape[0]
    assert d == D_MODEL and depth == 2 and s % 2048 == 0
    mod = _ada(c, ada_w, ada_b).reshape(depth, bsz, 6, 1, d)
    pos3 = positions.reshape(bsz, s, 1)
    fgain = final_norm.reshape(1, d)

    proj = _ret_proj(x, pos3, mod, 0, norm_mix[0].reshape(1, d), ret_w_in[0].astype(BF16))
    ret = _retention(proj)
    x = _out_proj(ret, ret_w_out[0].astype(BF16), x, mod, 0)
    w1 = mlp_w1.astype(BF16)
    w2 = mlp_w2.astype(BF16)
    x, *hs = _mlp(x, mod, 0, norm_mlp[0].reshape(1, d), w1, w2, fgain, final_norm=False,
                  next_gain=norm_mix[1].reshape(1, d))

    w_att = att_w_in[0].astype(BF16)
    outs, stats = [], []
    for gi, (_, dil) in enumerate(ATT_GROUPS):
        pos_g = positions.reshape(bsz, s // dil, dil).transpose(0, 2, 1).reshape(bsz, s, 1)
        qkv = _att_proj(hs[gi].reshape(bsz, s, d), pos_g, w_att, gi)
        o_g, st_g = _attention(qkv.reshape(bsz, dil, s // dil, 3 * ATT_WIDTH))
        outs.append(o_g)
        stats.append(st_g)
    x = _att_merge(outs, stats, att_w_out[0].astype(BF16), x, mod, 1)
    x = _mlp(x, mod, 1, norm_mlp[1].reshape(1, d), w1, w2, fgain, final_norm=True)
    return x
```

```python
import functools

import numpy as np
import jax
import jax.numpy as jnp
from jax import lax
from jax.experimental import pallas as pl
from jax.experimental.pallas import tpu as pltpu

F32 = jnp.float32
BF16 = jnp.bfloat16

EPS = 1e-6
D_MODEL = 2048
D_FF = 4 * D_MODEL
RET_HEADS = 8
RET_DK = D_MODEL // RET_HEADS
RET_DV = 2 * RET_DK
RET_QK = RET_HEADS * RET_DK
RET_V = RET_HEADS * RET_DV
RET_PROJ = 2 * RET_QK + 2 * RET_V
RET_THETA = 10000.0
RET_CHUNK = 256
ATT_HEADS = 16
ATT_HEAD_DIM = D_MODEL // ATT_HEADS
ATT_WIDTH = ATT_HEADS * ATT_HEAD_DIM
ATT_GROUPS = ((128, 1), (512, 4), (2048, 16))
ATT_BLOCK = 128
ROT_DIM = ATT_HEAD_DIM // 4
ROPE_THETA = 500000.0
LANES = 128
MXU_COLS = 256

VMEM_LIMIT_BYTES = 60 * 1024 * 1024


def _params(*sem):
    return pltpu.CompilerParams(dimension_semantics=sem, vmem_limit_bytes=VMEM_LIMIT_BYTES)


def _dot(a, b):
    return jnp.dot(a, b, preferred_element_type=F32)


def _dot_nt(a, b):
    return lax.dot_general(a, b, (((1,), (1,)), ((), ())), preferred_element_type=F32)


def _dot_tn(a, b):
    return lax.dot_general(a, b, (((0,), (0,)), ((), ())), preferred_element_type=F32)


def _norm_modulate(x, gain, shift, scale):
    ms = jnp.mean(x * x, axis=-1, keepdims=True)
    y = x * lax.rsqrt(ms + EPS) * gain
    return y * (1.0 + scale) + shift


def _gated_residual_matmul(a, w_ref, x_ref, gate_ref, y_ref):
    d = y_ref.shape[-1]
    ssq = jnp.zeros((y_ref.shape[0], 1), F32)
    for c in range(d // MXU_COLS):
        cs = slice(c * MXU_COLS, (c + 1) * MXU_COLS)
        y = x_ref[:, cs] + gate_ref[:, cs] * _dot(a, w_ref[:, cs])
        y_ref[:, cs] = y
        ssq = ssq + jnp.sum(y * y, axis=-1, keepdims=True)
    return ssq * (1.0 / d)


def _ada_kernel(ct_ref, w_ref, b_ref, o_ref):
    ct = ct_ref[...]
    cs = ct * jax.nn.sigmoid(ct)
    w = w_ref[...]
    for b in range(o_ref.shape[0]):
        o_ref[b:b + 1, :] = jnp.sum(w * cs[:, b:b + 1], axis=0, keepdims=True) + b_ref[...]


def _ada(c, ada_w, ada_b, tn=1024):
    depth, d, n = ada_w.shape
    bsz = c.shape[0]
    return pl.pallas_call(
        _ada_kernel,
        grid=(depth, n // tn),
        in_specs=[
            pl.BlockSpec((d, bsz), lambda l, j: (0, 0)),
            pl.BlockSpec((None, d, tn), lambda l, j: (l, 0, j)),
            pl.BlockSpec((None, 1, tn), lambda l, j: (l, 0, j)),
        ],
        out_specs=pl.BlockSpec((None, bsz, tn), lambda l, j: (l, 0, j)),
        out_shape=jax.ShapeDtypeStruct((depth, bsz, n), F32),
        compiler_params=_params("parallel", "parallel"),
        name="ada_mod",
    )(c.T, ada_w, ada_b.reshape(depth, 1, n))


def _mod_spec(layer, which):
    return pl.BlockSpec((None, None, None, 1, D_MODEL),
                        lambda b, *_: (layer, b, which, 0, 0))


def _row_spec(layer):
    return pl.BlockSpec((None, 1, D_MODEL), lambda *_: (layer, 0, 0))


def _ret_proj_kernel(x_ref, pos_ref, gain_ref, shift_ref, scale_ref, invf_ref, w_ref,
                     o_ref, h_ref, cos_ref, sin_ref, *, tn):
    j = pl.program_id(2)

    @pl.when(j == 0)
    def _():
        h = _norm_modulate(x_ref[...], gain_ref[...], shift_ref[...], scale_ref[...])
        h_ref[...] = h.astype(BF16)
        ang = pos_ref[...] * invf_ref[...]
        cos_ref[...] = jnp.cos(ang)
        sin_ref[...] = jnp.sin(ang)

    n_qk = 2 * RET_QK // tn
    kscale = jnp.where(j >= n_qk // 2, RET_DK ** -0.5, 1.0).astype(F32)
    rotated = j < n_qk
    c = jnp.where(rotated, cos_ref[...] * kscale, 1.0)
    s = jnp.where(rotated, sin_ref[...] * kscale, 0.0)
    half = RET_DK // 2
    h = h_ref[...]
    for hh in range(tn // RET_DK):
        lo = hh * RET_DK
        acc = _dot(h, w_ref[:, lo:lo + RET_DK])
        a1 = acc[:, :half]
        a2 = acc[:, half:]
        o_ref[:, lo:lo + half] = (a1 * c - a2 * s).astype(BF16)
        o_ref[:, lo + half:lo + RET_DK] = (a2 * c + a1 * s).astype(BF16)


def _ret_proj(x, pos, mod, layer, gains, w, tm=1024, tn=2048):
    bsz, s, d = x.shape
    n = w.shape[1]
    half = RET_DK // 2
    invf = np.power(np.float32(RET_THETA),
                    -np.arange(half, dtype=np.float32) * np.float32(2.0) / np.float32(RET_DK))
    invf = jnp.asarray(invf.astype(np.float32).reshape(1, half))
    return pl.pallas_call(
        functools.partial(_ret_proj_kernel, tn=tn),
        grid=(bsz, s // tm, n // tn),
        in_specs=[
            pl.BlockSpec((None, tm, d), lambda b, i, j: (b, i, 0)),
            pl.BlockSpec((None, tm, LANES), lambda b, i, j: (b, i, 0)),
            _row_spec(layer),
            _mod_spec(layer, 0),
            _mod_spec(layer, 1),
            pl.BlockSpec((1, half), lambda b, i, j: (0, 0)),
            pl.BlockSpec((d, tn), lambda b, i, j: (0, j)),
        ],
        out_specs=pl.BlockSpec((None, tm, tn), lambda b, i, j: (b, i, j)),
        out_shape=jax.ShapeDtypeStruct((bsz, s, n), BF16),
        scratch_shapes=[pltpu.VMEM((tm, d), BF16), pltpu.VMEM((tm, half), F32),
                        pltpu.VMEM((tm, half), F32)],
        compiler_params=_params("parallel", "parallel", "arbitrary"),
        name="ret_proj",
    )(x, pos, gains, mod, mod, invf, w)


def _ret_tables(chunk):
    lg = np.log1p(-np.exp2(-5.0 - np.arange(RET_HEADS, dtype=np.float64)))
    idx = np.arange(chunk, dtype=np.float64)
    diff = idx[:, None] - idx[None, :]
    decay = np.where(diff[None] >= 0, np.exp(np.maximum(diff, 0.0)[None] * lg[:, None, None]), 0.0)
    xi = np.exp((idx + 1.0)[None, :] * lg[:, None])
    zeta = np.exp((chunk - 1.0 - idx)[None, :] * lg[:, None])
    cd = np.exp(chunk * lg)
    return (jnp.asarray(decay.astype(np.float32)),
            jnp.asarray(xi.astype(np.float32)[:, :, None]),
            jnp.asarray(zeta.astype(np.float32)[:, :, None]),
            [float(v) for v in cd])


def _ret_kernel(q_ref, k_ref, v_ref, g_ref, decay_ref, xi_ref, zeta_ref, o_ref, state_ref,
                *, tt, chunk, cd):
    @pl.when(pl.program_id(1) == 0)
    def _():
        state_ref[...] = jnp.zeros_like(state_ref)

    for h in range(RET_HEADS):
        qs = slice(h * RET_DK, (h + 1) * RET_DK)
        vs = slice(h * RET_DV, (h + 1) * RET_DV)
        for c in range(tt // chunk):
            rows = slice(c * chunk, (c + 1) * chunk)
            qc = q_ref[rows, qs]
            kc = k_ref[rows, qs]
            vc = v_ref[rows, vs]
            st = state_ref[h]
            scores = _dot_nt(qc, kc) * decay_ref[h]
            inner = _dot(scores.astype(BF16), vc)
            cross = _dot(qc, st.astype(BF16)) * xi_ref[h]
            o = inner + cross
            kz = (kc.astype(F32) * zeta_ref[h]).astype(BF16)
            state_ref[h] = cd[h] * st + _dot_tn(kz, vc)
            mu = jnp.mean(o, axis=-1, keepdims=True)
            oc = o - mu
            var = jnp.mean(oc * oc, axis=-1, keepdims=True)
            on = oc * lax.rsqrt(var + EPS)
            g = g_ref[rows, vs].astype(F32)
            o_ref[rows, vs] = (g * jax.nn.sigmoid(g) * on).astype(BF16)


def _retention(proj, tt=512, chunk=RET_CHUNK):
    bsz, s, _ = proj.shape
    decay, xi, zeta, cd = _ret_tables(chunk)
    const3 = lambda b, n: (0, 0, 0)
    return pl.pallas_call(
        functools.partial(_ret_kernel, tt=tt, chunk=chunk, cd=cd),
        grid=(bsz, s // tt),
        in_specs=[
            pl.BlockSpec((None, tt, RET_QK), lambda b, n: (b, n, 0)),
            pl.BlockSpec((None, tt, RET_QK), lambda b, n: (b, n, 1)),
            pl.BlockSpec((None, tt, RET_V), lambda b, n: (b, n, 1)),
            pl.BlockSpec((None, tt, RET_V), lambda b, n: (b, n, 2)),
            pl.BlockSpec((RET_HEADS, chunk, chunk), const3),
            pl.BlockSpec((RET_HEADS, chunk, 1), const3),
            pl.BlockSpec((RET_HEADS, chunk, 1), const3),
        ],
        out_specs=pl.BlockSpec((None, tt, RET_V), lambda b, n: (b, n, 0)),
        out_shape=jax.ShapeDtypeStruct((bsz, s, RET_V), BF16),
        scratch_shapes=[pltpu.VMEM((RET_HEADS, RET_DK, RET_DV), F32)],
        compiler_params=_params("parallel", "arbitrary"),
        name="retention",
    )(proj, proj, proj, proj, decay, xi, zeta)


def _emit_next_input(y_ref, ms, gain_ref, shift_ref, scale_ref, h_ref):
    g2 = gain_ref[...] * (1.0 + scale_ref[...])
    h_ref[...] = (y_ref[...] * lax.rsqrt(ms + EPS) * g2 + shift_ref[...]).astype(BF16)


def _out_proj_kernel(a_ref, w_ref, x_ref, gate_ref, ngain_ref, nshift_ref, nscale_ref,
                     y_ref, h_ref):
    ms = _gated_residual_matmul(a_ref[...], w_ref, x_ref, gate_ref, y_ref)
    _emit_next_input(y_ref, ms, ngain_ref, nshift_ref, nscale_ref, h_ref)


def _out_proj(a, w, x, mod, layer, mlp_gains, tm=512):
    bsz, s, k = a.shape
    d = w.shape[1]
    tile = lambda b, i: (b, i, 0)
    return pl.pallas_call(
        _out_proj_kernel,
        grid=(bsz, s // tm),
        in_specs=[
            pl.BlockSpec((None, tm, k), tile),
            pl.BlockSpec((k, d), lambda b, i: (0, 0)),
            pl.BlockSpec((None, tm, d), tile),
            _mod_spec(layer, 2),
            _row_spec(layer),
            _mod_spec(layer, 3),
            _mod_spec(layer, 4),
        ],
        out_specs=[pl.BlockSpec((None, tm, d), tile), pl.BlockSpec((None, tm, d), tile)],
        out_shape=[jax.ShapeDtypeStruct((bsz, s, d), F32), jax.ShapeDtypeStruct((bsz, s, d), BF16)],
        compiler_params=_params("parallel", "parallel"),
        name="out_proj",
    )(a, w, x, mod, mlp_gains, mod, mod)


def _mlp_up_kernel(h_ref, w_ref, o_ref):
    h = h_ref[...]
    cw = 2 * MXU_COLS
    for c in range(o_ref.shape[-1] // cw):
        cs = slice(c * cw, (c + 1) * cw)
        a = jnp.maximum(_dot(h, w_ref[:, cs]), 0.0)
        o_ref[:, cs] = (a * a).astype(BF16)


def _mlp_up(h, w1, layer, tm=256):
    bsz, s, d = h.shape
    dff = w1.shape[2]
    tile = lambda b, i: (b, i, 0)
    return pl.pallas_call(
        _mlp_up_kernel,
        grid=(bsz, s // tm),
        in_specs=[pl.BlockSpec((None, tm, d), tile),
                  pl.BlockSpec((None, d, dff), lambda b, i: (layer, 0, 0))],
        out_specs=pl.BlockSpec((None, tm, dff), tile),
        out_shape=jax.ShapeDtypeStruct((bsz, s, dff), BF16),
        compiler_params=_params("parallel", "parallel"),
        name="mlp_up",
    )(h, w1)


def _mlp_down_kernel(*refs, final_norm):
    if final_norm:
        a_ref, w_ref, x_ref, gate_ref, fgain_ref, y_ref = refs
    else:
        a_ref, w_ref, x_ref, gate_ref, ngain_ref, nshift_ref, nscale_ref, y_ref, h_ref = refs
    ms = _gated_residual_matmul(a_ref[...], w_ref, x_ref, gate_ref, y_ref)
    if final_norm:
        y_ref[...] = y_ref[...] * lax.rsqrt(ms + EPS) * fgain_ref[...]
    else:
        _emit_next_input(y_ref, ms, ngain_ref, nshift_ref, nscale_ref, h_ref)


def _mlp_down(a, w2, x, mod, layer, fgain=None, mix_gains=None, tm=256):
    bsz, s, dff = a.shape
    d = x.shape[-1]
    final_norm = fgain is not None
    tile = lambda b, i: (b, i, 0)
    in_specs = [
        pl.BlockSpec((None, tm, dff), tile),
        pl.BlockSpec((None, dff, d), lambda b, i: (layer, 0, 0)),
        pl.BlockSpec((None, tm, d), tile),
        _mod_spec(layer, 5),
    ]
    operands = [a, w2, x, mod]
    if final_norm:
        in_specs.append(pl.BlockSpec((1, d), lambda b, i: (0, 0)))
        operands.append(fgain)
        out_specs = pl.BlockSpec((None, tm, d), tile)
        out_shape = jax.ShapeDtypeStruct((bsz, s, d), F32)
    else:
        in_specs += [_row_spec(layer + 1), _mod_spec(layer + 1, 0), _mod_spec(layer + 1, 1)]
        operands += [mix_gains, mod, mod]
        out_specs = [pl.BlockSpec((None, tm, d), tile), pl.BlockSpec((None, tm, d), tile)]
        out_shape = [jax.ShapeDtypeStruct((bsz, s, d), F32), jax.ShapeDtypeStruct((bsz, s, d), BF16)]
    return pl.pallas_call(
        functools.partial(_mlp_down_kernel, final_norm=final_norm),
        grid=(bsz, s // tm),
        in_specs=in_specs,
        out_specs=out_specs,
        out_shape=out_shape,
        compiler_params=_params("parallel", "parallel"),
        name="mlp_down",
    )(*operands)


def _perm_matrix(tm, dil):
    t = np.arange(tm)
    p = (t % dil) * (tm // dil) + t // dil
    m = np.zeros((tm, tm), np.float32)
    m[p, t] = 1.0
    return m


def _att_proj_kernel(*refs, dil, tm):
    if dil > 1:
        h_ref, perm_ref, pos_ref, invf_ref, w_ref, o_ref = refs
        h = _dot(perm_ref[...], h_ref[...]).astype(BF16)
    else:
        h_ref, pos_ref, invf_ref, w_ref, o_ref = refs
        h = h_ref[...]
    half = ROT_DIM // 2
    ang = pos_ref[...].reshape(tm, LANES) * invf_ref[...]
    first_half = lax.broadcasted_iota(jnp.int32, ang.shape, 1) < half
    cos = jnp.cos(ang)
    sin = jnp.where(first_half, -jnp.sin(ang), jnp.sin(ang))
    qscale = ATT_HEAD_DIM ** -0.5
    rot = ((cos * qscale, sin * qscale), (cos, sin))
    for cb in range(3 * ATT_WIDTH // MXU_COLS):
        kind = cb * MXU_COLS // ATT_WIDTH
        acc = _dot(h, w_ref[:, cb * MXU_COLS:(cb + 1) * MXU_COLS])
        for hh in range(MXU_COLS // LANES):
            blk = acc[:, hh * LANES:(hh + 1) * LANES]
            if kind < 2:
                c, s = rot[kind]
                partner = jnp.where(first_half, pltpu.roll(blk, LANES - half, 1),
                                    pltpu.roll(blk, half, 1))
                blk = blk * c + partner * s
            lo = cb * MXU_COLS + hh * LANES
            o_ref[:, :, lo:lo + LANES] = blk.astype(BF16).reshape(dil, tm // dil, LANES)


def _att_proj(h, pos, w, group, dil, tm=512):
    bsz, s, d = h.shape
    tn = 3 * ATT_WIDTH
    half = ROT_DIM // 2
    invf = np.power(np.float32(ROPE_THETA),
                    -np.arange(half, dtype=np.float32) * np.float32(2.0) / np.float32(ROT_DIM))
    invf_full = np.zeros((1, LANES), np.float32)
    invf_full[0, :half] = invf
    invf_full[0, half:ROT_DIM] = invf
    in_specs = [pl.BlockSpec((None, tm, d), lambda b, i: (b, i, 0))]
    operands = [h]
    if dil > 1:
        in_specs.append(pl.BlockSpec((tm, tm), lambda b, i: (0, 0)))
        operands.append(jnp.asarray(_perm_matrix(tm, dil), BF16))
    in_specs += [
        pl.BlockSpec((None, dil, tm // dil, LANES), lambda b, i: (b, 0, i, 0)),
        pl.BlockSpec((1, LANES), lambda b, i: (0, 0)),
        pl.BlockSpec((d, tn), lambda b, i: (0, group)),
    ]
    operands += [pos, jnp.asarray(invf_full), w]
    return pl.pallas_call(
        functools.partial(_att_proj_kernel, dil=dil, tm=tm),
        grid=(bsz, s // tm),
        in_specs=in_specs,
        out_specs=pl.BlockSpec((None, dil, tm // dil, tn), lambda b, i: (b, 0, i, 0)),
        out_shape=jax.ShapeDtypeStruct((bsz, dil, s // dil, tn), BF16),
        compiler_params=_params("parallel", "parallel"),
        name=f"att_proj_g{group}",
    )(*operands)


def _att_kernel(q_ref, kc_ref, kp_ref, vc_ref, vp_ref, o_ref, st_ref, *, tq):
    n = pl.program_id(2)
    blk = ATT_BLOCK
    row = lax.broadcasted_iota(jnp.int32, (blk, 2 * blk), 0)
    col = lax.broadcasted_iota(jnp.int32, (blk, 2 * blk), 1)
    band = (col >= row) & (col <= row + blk)
    band_first = band & ((col >= blk) | (n > 0))
    lane = lax.broadcasted_iota(jnp.int32, (blk, LANES), 1)
    for i in range(tq // blk):
        rows = slice(i * blk, (i + 1) * blk)
        stats = jnp.zeros((blk, LANES), F32)
        for h in range(ATT_HEADS):
            hs = slice(h * ATT_HEAD_DIM, (h + 1) * ATT_HEAD_DIM)
            qi = q_ref[rows, hs]
            if i == 0:
                kk = jnp.concatenate([kp_ref[:, hs], kc_ref[0:blk, hs]], axis=0)
                vv = jnp.concatenate([vp_ref[:, hs], vc_ref[0:blk, hs]], axis=0)
                mask = band_first
            else:
                kk = kc_ref[(i - 1) * blk:(i + 1) * blk, hs]
                vv = vc_ref[(i - 1) * blk:(i + 1) * blk, hs]
                mask = band
            s = jnp.where(mask, _dot_nt(qi, kk), -jnp.inf)
            m = jnp.max(s, axis=-1, keepdims=True)
            p = jnp.exp(s - m)
            l = jnp.sum(p, axis=-1, keepdims=True)
            o = _dot(p.astype(BF16), vv) / l
            o_ref[rows, hs] = o.astype(BF16)
            stats = jnp.where(lane == h, m + jnp.log(l), stats)
        hi = stats.astype(BF16)
        r1 = stats - hi.astype(F32)
        mid = r1.astype(BF16)
        lo = (r1 - mid.astype(F32)).astype(BF16)
        st_ref[rows, 0:LANES] = hi
        st_ref[rows, LANES:2 * LANES] = mid
        st_ref[rows, 2 * LANES:3 * LANES] = lo


def _attention(qkv, tq=256):
    bsz, dil, ln, _ = qkv.shape
    tq = min(tq, ln)
    per = tq // ATT_BLOCK
    cur = lambda c: pl.BlockSpec((None, None, tq, ATT_WIDTH), lambda b, r, n: (b, r, n, c))
    prev = lambda c: pl.BlockSpec((None, None, ATT_BLOCK, ATT_WIDTH),
                                  lambda b, r, n: (b, r, jnp.maximum(n * per - 1, 0), c))
    return pl.pallas_call(
        functools.partial(_att_kernel, tq=tq),
        grid=(bsz, dil, ln // tq),
        in_specs=[cur(0), cur(1), prev(1), cur(2), prev(2)],
        out_specs=[pl.BlockSpec((None, None, tq, ATT_WIDTH), lambda b, r, n: (b, r, n, 0)),
                   pl.BlockSpec((None, None, tq, 3 * LANES), lambda b, r, n: (b, r, n, 0))],
        out_shape=[jax.ShapeDtypeStruct((bsz, dil, ln, ATT_WIDTH), BF16),
                   jax.ShapeDtypeStruct((bsz, dil, ln, 3 * LANES), BF16)],
        compiler_params=_params("parallel", "parallel", "parallel"),
        name=f"attention_d{dil}",
    )(qkv, qkv, qkv, qkv, qkv)


def _att_merge_kernel(o1_ref, o4_ref, o16_ref, s1_ref, s4_ref, s16_ref, pt4_ref, pt16_ref,
                      ex_ref, w_ref, x_ref, gate_ref, ngain_ref, nshift_ref, nscale_ref,
                      y_ref, h_ref, lhs_ref, *, tm):
    def lse_of(pieces):
        return pieces[:, 0:LANES] + pieces[:, LANES:2 * LANES] + pieces[:, 2 * LANES:3 * LANES]

    lse1 = lse_of(s1_ref[...].astype(F32))
    lse4 = lse_of(_dot(pt4_ref[...], s4_ref[...].reshape(tm, 3 * LANES)))
    lse16 = lse_of(_dot(pt16_ref[...], s16_ref[...].reshape(tm, 3 * LANES)))
    m = jnp.maximum(jnp.maximum(lse1, lse4), lse16)
    e1 = jnp.exp(lse1 - m)
    e4 = jnp.exp(lse4 - m)
    e16 = jnp.exp(lse16 - m)
    inv = 1.0 / (e1 + e4 + e16)

    def split(w):
        hi = w.astype(BF16)
        lo = (w - hi.astype(F32)).astype(BF16)
        return jnp.concatenate([hi, lo], axis=1)

    w1, w4, w16 = split(e1 * inv), split(e4 * inv), split(e16 * inv)
    cw = 4 * ATT_HEAD_DIM
    for cb in range(ATT_WIDTH // cw):
        cs = slice(cb * cw, (cb + 1) * cw)
        ex = ex_ref[:, cs]
        a1 = o1_ref[:, cs].astype(F32)
        a4 = _dot(pt4_ref[...], o4_ref[:, :, cs].reshape(tm, cw))
        a16 = _dot(pt16_ref[...], o16_ref[:, :, cs].reshape(tm, cw))
        mix = _dot(w1, ex) * a1 + _dot(w4, ex) * a4 + _dot(w16, ex) * a16
        lhs_ref[:, cs] = mix.astype(BF16)
    ms = _gated_residual_matmul(lhs_ref[...], w_ref, x_ref, gate_ref, y_ref)
    _emit_next_input(y_ref, ms, ngain_ref, nshift_ref, nscale_ref, h_ref)


def _att_merge(outs, stats, w, x, mod, layer, mlp_gains, tm=256):
    bsz, s, d = x.shape
    in_specs, operands = [], []
    for arr in list(outs) + list(stats):
        dil, width = arr.shape[1], arr.shape[3]
        if dil == 1:
            in_specs.append(pl.BlockSpec((None, None, tm, width), lambda b, i: (b, 0, i, 0)))
        else:
            in_specs.append(pl.BlockSpec((None, dil, tm // dil, width), lambda b, i: (b, 0, i, 0)))
        operands.append(arr)
    expand = np.zeros((2 * LANES, ATT_WIDTH), np.float32)
    for h in range(ATT_HEADS):
        expand[h, h * ATT_HEAD_DIM:(h + 1) * ATT_HEAD_DIM] = 1.0
        expand[LANES + h, h * ATT_HEAD_DIM:(h + 1) * ATT_HEAD_DIM] = 1.0
    const2 = lambda b, i: (0, 0)
    tile = lambda b, i: (b, i, 0)
    in_specs += [
        pl.BlockSpec((tm, tm), const2),
        pl.BlockSpec((tm, tm), const2),
        pl.BlockSpec((2 * LANES, ATT_WIDTH), const2),
        pl.BlockSpec((ATT_WIDTH, d), const2),
        pl.BlockSpec((None, tm, d), tile),
        _mod_spec(layer, 2),
        _row_spec(layer),
        _mod_spec(layer, 3),
        _mod_spec(layer, 4),
    ]
    operands += [jnp.asarray(_perm_matrix(tm, 4).T, BF16), jnp.asarray(_perm_matrix(tm, 16).T, BF16),
                 jnp.asarray(expand, BF16), w, x, mod, mlp_gains, mod, mod]
    return pl.pallas_call(
        functools.partial(_att_merge_kernel, tm=tm),
        grid=(bsz, s // tm),
        in_specs=in_specs,
        out_specs=[pl.BlockSpec((None, tm, d), tile), pl.BlockSpec((None, tm, d), tile)],
        out_shape=[jax.ShapeDtypeStruct((bsz, s, d), F32), jax.ShapeDtypeStruct((bsz, s, d), BF16)],
        scratch_shapes=[pltpu.VMEM((tm, ATT_WIDTH), BF16)],
        compiler_params=_params("parallel", "parallel"),
        name="att_merge",
    )(*operands)


def kernel(x, c, positions, ada_w, ada_b, norm_mix, norm_mlp, ret_w_in, ret_w_out,
           att_w_in, att_w_out, mlp_w1, mlp_w2, final_norm):
    bsz, s, d = x.shape
    depth = ada_w.shape[0]
    assert d == D_MODEL and depth == 2 and s % 2048 == 0
    mod = _ada(c, ada_w, ada_b).reshape(depth, bsz, 6, 1, d)
    mix_gains = norm_mix.reshape(depth, 1, d)
    mlp_gains = norm_mlp.reshape(depth, 1, d)
    w1 = mlp_w1.astype(BF16)
    w2 = mlp_w2.astype(BF16)
    lane_pos = lambda p: jnp.broadcast_to(p.astype(F32)[..., None], p.shape + (LANES,))

    proj = _ret_proj(x, lane_pos(positions), mod, 0, mix_gains, ret_w_in[0].astype(BF16))
    ret = _retention(proj)
    x, h = _out_proj(ret, ret_w_out[0].astype(BF16), x, mod, 0, mlp_gains)
    x, h = _mlp_down(_mlp_up(h, w1, 0), w2, x, mod, 0, mix_gains=mix_gains)

    w_att = att_w_in[0].astype(BF16)
    outs, stats = [], []
    for gi, (_, dil) in enumerate(ATT_GROUPS):
        pos_g = lane_pos(positions.reshape(bsz, s // dil, dil).transpose(0, 2, 1))
        o_g, st_g = _attention(_att_proj(h, pos_g, w_att, gi, dil))
        outs.append(o_g)
        stats.append(st_g)
    x, h = _att_merge(outs, stats, att_w_out[0].astype(BF16), x, mod, 1, mlp_gains)
    return _mlp_down(_mlp_up(h, w1, 1), w2, x, mod, 1, fgain=final_norm.reshape(1, d))
```

```python
import functools

import numpy as np
import jax
import jax.numpy as jnp
from jax import lax
from jax.experimental import pallas as pl
from jax.experimental.pallas import tpu as pltpu

F32 = jnp.float32
BF16 = jnp.bfloat16

EPS = 1e-6
D_MODEL = 2048
D_FF = 4 * D_MODEL
RET_HEADS = 8
RET_DK = D_MODEL // RET_HEADS
RET_DV = 2 * RET_DK
RET_QK = RET_HEADS * RET_DK
RET_V = RET_HEADS * RET_DV
RET_PROJ = 2 * RET_QK + 2 * RET_V
RET_THETA = 10000.0
RET_CHUNK = 256
ATT_HEADS = 16
ATT_HEAD_DIM = D_MODEL // ATT_HEADS
ATT_WIDTH = ATT_HEADS * ATT_HEAD_DIM
ATT_GROUPS = ((128, 1), (512, 4), (2048, 16))
ATT_BLOCK = 128
ROT_DIM = ATT_HEAD_DIM // 4
ROPE_THETA = 500000.0
LANES = 128
MXU_COLS = 256

VMEM_LIMIT_BYTES = 60 * 1024 * 1024


def _params(*sem):
    return pltpu.CompilerParams(dimension_semantics=sem, vmem_limit_bytes=VMEM_LIMIT_BYTES)


def _dot(a, b):
    return jnp.dot(a, b, preferred_element_type=F32)


def _dot_nt(a, b):
    return lax.dot_general(a, b, (((1,), (1,)), ((), ())), preferred_element_type=F32)


def _dot_tn(a, b):
    return lax.dot_general(a, b, (((0,), (0,)), ((), ())), preferred_element_type=F32)


def _norm_modulate(x, gain, shift, scale):
    ms = jnp.mean(x * x, axis=-1, keepdims=True)
    y = x * lax.rsqrt(ms + EPS) * gain
    return y * (1.0 + scale) + shift


def _gated_residual_matmul(a, w_ref, x_ref, gate_ref, y_ref):
    d = y_ref.shape[-1]
    ssq = jnp.zeros((y_ref.shape[0], 1), F32)
    for c in range(d // MXU_COLS):
        cs = slice(c * MXU_COLS, (c + 1) * MXU_COLS)
        y = x_ref[:, cs] + gate_ref[:, cs] * _dot(a, w_ref[:, cs])
        y_ref[:, cs] = y
        ssq = ssq + jnp.sum(y * y, axis=-1, keepdims=True)
    return ssq * (1.0 / d)


def _ada_kernel(ct_ref, w_ref, b_ref, o_ref):
    ct = ct_ref[...]
    cs = ct * jax.nn.sigmoid(ct)
    w = w_ref[...]
    for b in range(o_ref.shape[0]):
        o_ref[b:b + 1, :] = jnp.sum(w * cs[:, b:b + 1], axis=0, keepdims=True) + b_ref[...]


def _ada(c, ada_w, ada_b, tn=1024):
    depth, d, n = ada_w.shape
    bsz = c.shape[0]
    return pl.pallas_call(
        _ada_kernel,
        grid=(depth, n // tn),
        in_specs=[
            pl.BlockSpec((d, bsz), lambda l, j: (0, 0)),
            pl.BlockSpec((None, d, tn), lambda l, j: (l, 0, j)),
            pl.BlockSpec((None, 1, tn), lambda l, j: (l, 0, j)),
        ],
        out_specs=pl.BlockSpec((None, bsz, tn), lambda l, j: (l, 0, j)),
        out_shape=jax.ShapeDtypeStruct((depth, bsz, n), F32),
        compiler_params=_params("parallel", "parallel"),
        name="ada_mod",
    )(c.T, ada_w, ada_b.reshape(depth, 1, n))


def _mod_spec(layer, which):
    return pl.BlockSpec((None, None, None, 1, D_MODEL),
                        lambda b, *_: (layer, b, which, 0, 0))


def _row_spec(layer):
    return pl.BlockSpec((None, 1, D_MODEL), lambda *_: (layer, 0, 0))


def _ret_proj_kernel(x_ref, pos_ref, gain_ref, shift_ref, scale_ref, invf_ref, w_ref,
                     o_ref, h_ref, cos_ref, sin_ref, *, tn):
    j = pl.program_id(2)

    @pl.when(j == 0)
    def _():
        h = _norm_modulate(x_ref[...], gain_ref[...], shift_ref[...], scale_ref[...])
        h_ref[...] = h.astype(BF16)
        ang = pos_ref[...] * invf_ref[...]
        cos_ref[...] = jnp.cos(ang)
        sin_ref[...] = jnp.sin(ang)

    n_qk = 2 * RET_QK // tn
    kscale = jnp.where(j >= n_qk // 2, RET_DK ** -0.5, 1.0).astype(F32)
    rotated = j < n_qk
    c = jnp.where(rotated, cos_ref[...] * kscale, 1.0)
    s = jnp.where(rotated, sin_ref[...] * kscale, 0.0)
    half = RET_DK // 2
    h = h_ref[...]
    for hh in range(tn // RET_DK):
        lo = hh * RET_DK
        acc = _dot(h, w_ref[:, lo:lo + RET_DK])
        a1 = acc[:, :half]
        a2 = acc[:, half:]
        o_ref[:, lo:lo + half] = (a1 * c - a2 * s).astype(BF16)
        o_ref[:, lo + half:lo + RET_DK] = (a2 * c + a1 * s).astype(BF16)


def _ret_proj(x, pos, mod, layer, gains, w, tm=1024, tn=2048):
    bsz, s, d = x.shape
    n = w.shape[1]
    half = RET_DK // 2
    invf = np.power(np.float32(RET_THETA),
                    -np.arange(half, dtype=np.float32) * np.float32(2.0) / np.float32(RET_DK))
    invf = jnp.asarray(invf.astype(np.float32).reshape(1, half))
    return pl.pallas_call(
        functools.partial(_ret_proj_kernel, tn=tn),
        grid=(bsz, s // tm, n // tn),
        in_specs=[
            pl.BlockSpec((None, tm, d), lambda b, i, j: (b, i, 0)),
            pl.BlockSpec((None, tm, LANES), lambda b, i, j: (b, i, 0)),
            _row_spec(layer),
            _mod_spec(layer, 0),
            _mod_spec(layer, 1),
            pl.BlockSpec((1, half), lambda b, i, j: (0, 0)),
            pl.BlockSpec((d, tn), lambda b, i, j: (0, j)),
        ],
        out_specs=pl.BlockSpec((None, tm, tn), lambda b, i, j: (b, i, j)),
        out_shape=jax.ShapeDtypeStruct((bsz, s, n), BF16),
        scratch_shapes=[pltpu.VMEM((tm, d), BF16), pltpu.VMEM((tm, half), F32),
                        pltpu.VMEM((tm, half), F32)],
        compiler_params=_params("parallel", "parallel", "arbitrary"),
        name="ret_proj",
    )(x, pos, gains, mod, mod, invf, w)


def _ret_tables(chunk):
    lg = np.log1p(-np.exp2(-5.0 - np.arange(RET_HEADS, dtype=np.float64)))
    idx = np.arange(chunk, dtype=np.float64)
    diff = idx[:, None] - idx[None, :]
    decay = np.where(diff[None] >= 0, np.exp(np.maximum(diff, 0.0)[None] * lg[:, None, None]), 0.0)
    xi = np.exp((idx + 1.0)[None, :] * lg[:, None])
    zeta = np.exp((chunk - 1.0 - idx)[None, :] * lg[:, None])
    cd = np.exp(chunk * lg)
    return (jnp.asarray(decay.astype(np.float32)),
            jnp.asarray(xi.astype(np.float32)[:, :, None]),
            jnp.asarray(zeta.astype(np.float32)[:, :, None]),
            [float(v) for v in cd])


def _ret_kernel(q_ref, k_ref, v_ref, decay_ref, xi_ref, zeta_ref, o_ref, state_ref,
                *, tt, chunk, cd):
    @pl.when(pl.program_id(1) == 0)
    def _():
        state_ref[...] = jnp.zeros_like(state_ref)

    for h in range(RET_HEADS):
        qs = slice(h * RET_DK, (h + 1) * RET_DK)
        vs = slice(h * RET_DV, (h + 1) * RET_DV)
        for c in range(tt // chunk):
            rows = slice(c * chunk, (c + 1) * chunk)
            qc = q_ref[rows, qs]
            kc = k_ref[rows, qs]
            vc = v_ref[rows, vs]
            st = state_ref[h]
            scores = _dot_nt(qc, kc) * decay_ref[h]
            inner = _dot(scores.astype(BF16), vc)
            cross = _dot(qc, st.astype(BF16)) * xi_ref[h]
            o_ref[rows, vs] = (inner + cross).astype(BF16)
            kz = (kc.astype(F32) * zeta_ref[h]).astype(BF16)
            state_ref[h] = cd[h] * st + _dot_tn(kz, vc)


def _retention(proj, tt=512, chunk=RET_CHUNK):
    bsz, s, _ = proj.shape
    decay, xi, zeta, cd = _ret_tables(chunk)
    const3 = lambda b, n: (0, 0, 0)
    return pl.pallas_call(
        functools.partial(_ret_kernel, tt=tt, chunk=chunk, cd=cd),
        grid=(bsz, s // tt),
        in_specs=[
            pl.BlockSpec((None, tt, RET_QK), lambda b, n: (b, n, 0)),
            pl.BlockSpec((None, tt, RET_QK), lambda b, n: (b, n, 1)),
            pl.BlockSpec((None, tt, RET_V), lambda b, n: (b, n, 1)),
            pl.BlockSpec((RET_HEADS, chunk, chunk), const3),
            pl.BlockSpec((RET_HEADS, chunk, 1), const3),
            pl.BlockSpec((RET_HEADS, chunk, 1), const3),
        ],
        out_specs=pl.BlockSpec((None, tt, RET_V), lambda b, n: (b, n, 0)),
        out_shape=jax.ShapeDtypeStruct((bsz, s, RET_V), BF16),
        scratch_shapes=[pltpu.VMEM((RET_HEADS, RET_DK, RET_DV), F32)],
        compiler_params=_params("parallel", "arbitrary"),
        name="retention",
    )(proj, proj, proj, decay, xi, zeta)


def _emit_next_input(y_ref, ms, gain_ref, shift_ref, scale_ref, h_ref):
    g2 = gain_ref[...] * (1.0 + scale_ref[...])
    h_ref[...] = (y_ref[...] * lax.rsqrt(ms + EPS) * g2 + shift_ref[...]).astype(BF16)


def _gated_head(o_ref, g_ref, h):
    vs = slice(h * RET_DV, (h + 1) * RET_DV)
    o = o_ref[:, vs].astype(F32)
    mu = jnp.mean(o, axis=-1, keepdims=True)
    oc = o - mu
    var = jnp.mean(oc * oc, axis=-1, keepdims=True)
    g = g_ref[:, vs].astype(F32)
    return (g * jax.nn.sigmoid(g) * (oc * lax.rsqrt(var + EPS))).astype(BF16)


def _out_proj_kernel(o_ref, g_ref, w_ref, x_ref, gate_ref, ngain_ref, nshift_ref, nscale_ref,
                     y_ref, h_ref):
    last = RET_HEADS - 1
    for h in range(last):
        contrib = _dot(_gated_head(o_ref, g_ref, h), w_ref[h * RET_DV:(h + 1) * RET_DV, :])
        if h == 0:
            y_ref[...] = contrib
        else:
            y_ref[...] += contrib
    d = y_ref.shape[-1]
    a = _gated_head(o_ref, g_ref, last)
    ssq = jnp.zeros((y_ref.shape[0], 1), F32)
    for c in range(d // MXU_COLS):
        cs = slice(c * MXU_COLS, (c + 1) * MXU_COLS)
        acc = y_ref[:, cs] + _dot(a, w_ref[last * RET_DV:, cs])
        y = x_ref[:, cs] + gate_ref[:, cs] * acc
        y_ref[:, cs] = y
        ssq = ssq + jnp.sum(y * y, axis=-1, keepdims=True)
    _emit_next_input(y_ref, ssq * (1.0 / d), ngain_ref, nshift_ref, nscale_ref, h_ref)


def _out_proj(o, proj, w, x, mod, layer, mlp_gains, tm=256):
    bsz, s, k = o.shape
    d = w.shape[1]
    tile = lambda b, i: (b, i, 0)
    return pl.pallas_call(
        _out_proj_kernel,
        grid=(bsz, s // tm),
        in_specs=[
            pl.BlockSpec((None, tm, k), tile),
            pl.BlockSpec((None, tm, k), lambda b, i: (b, i, 2)),
            pl.BlockSpec((k, d), lambda b, i: (0, 0)),
            pl.BlockSpec((None, tm, d), tile),
            _mod_spec(layer, 2),
            _row_spec(layer),
            _mod_spec(layer, 3),
            _mod_spec(layer, 4),
        ],
        out_specs=[pl.BlockSpec((None, tm, d), tile), pl.BlockSpec((None, tm, d), tile)],
        out_shape=[jax.ShapeDtypeStruct((bsz, s, d), F32), jax.ShapeDtypeStruct((bsz, s, d), BF16)],
        compiler_params=_params("parallel", "parallel"),
        name="out_proj",
    )(o, proj, w, x, mod, mlp_gains, mod, mod)


def _mlp_up_kernel(h_ref, w_ref, o_ref):
    h = h_ref[...]
    cw = 2 * MXU_COLS
    for c in range(o_ref.shape[-1] // cw):
        cs = slice(c * cw, (c + 1) * cw)
        a = jnp.maximum(_dot(h, w_ref[:, cs]), 0.0)
        o_ref[:, cs] = (a * a).astype(BF16)


def _mlp_up(h, w1, layer, tm=256):
    bsz, s, d = h.shape
    dff = w1.shape[2]
    tile = lambda b, i: (b, i, 0)
    return pl.pallas_call(
        _mlp_up_kernel,
        grid=(bsz, s // tm),
        in_specs=[pl.BlockSpec((None, tm, d), tile),
                  pl.BlockSpec((None, d, dff), lambda b, i: (layer, 0, 0))],
        out_specs=pl.BlockSpec((None, tm, dff), tile),
        out_shape=jax.ShapeDtypeStruct((bsz, s, dff), BF16),
        compiler_params=_params("parallel", "parallel"),
        name="mlp_up",
    )(h, w1)


def _mlp_down_kernel(*refs, final_norm):
    if final_norm:
        a_ref, w_ref, x_ref, gate_ref, fgain_ref, y_ref = refs
    else:
        a_ref, w_ref, x_ref, gate_ref, ngain_ref, nshift_ref, nscale_ref, y_ref, h_ref = refs
    ms = _gated_residual_matmul(a_ref[...], w_ref, x_ref, gate_ref, y_ref)
    if final_norm:
        y_ref[...] = y_ref[...] * lax.rsqrt(ms + EPS) * fgain_ref[...]
    else:
        _emit_next_input(y_ref, ms, ngain_ref, nshift_ref, nscale_ref, h_ref)


def _mlp_down(a, w2, x, mod, layer, fgain=None, mix_gains=None, tm=256):
    bsz, s, dff = a.shape
    d = x.shape[-1]
    final_norm = fgain is not None
    tile = lambda b, i: (b, i, 0)
    in_specs = [
        pl.BlockSpec((None, tm, dff), tile),
        pl.BlockSpec((None, dff, d), lambda b, i: (layer, 0, 0)),
        pl.BlockSpec((None, tm, d), tile),
        _mod_spec(layer, 5),
    ]
    operands = [a, w2, x, mod]
    if final_norm:
        in_specs.append(pl.BlockSpec((1, d), lambda b, i: (0, 0)))
        operands.append(fgain)
        out_specs = pl.BlockSpec((None, tm, d), tile)
        out_shape = jax.ShapeDtypeStruct((bsz, s, d), F32)
    else:
        in_specs += [_row_spec(layer + 1), _mod_spec(layer + 1, 0), _mod_spec(layer + 1, 1)]
        operands += [mix_gains, mod, mod]
        out_specs = [pl.BlockSpec((None, tm, d), tile), pl.BlockSpec((None, tm, d), tile)]
        out_shape = [jax.ShapeDtypeStruct((bsz, s, d), F32), jax.ShapeDtypeStruct((bsz, s, d), BF16)]
    return pl.pallas_call(
        functools.partial(_mlp_down_kernel, final_norm=final_norm),
        grid=(bsz, s // tm),
        in_specs=in_specs,
        out_specs=out_specs,
        out_shape=out_shape,
        compiler_params=_params("parallel", "parallel"),
        name="mlp_down",
    )(*operands)


def _perm_matrix(tm, dil):
    t = np.arange(tm)
    p = (t % dil) * (tm // dil) + t // dil
    m = np.zeros((tm, tm), np.float32)
    m[p, t] = 1.0
    return m


def _att_proj_kernel(*refs, dil, tm):
    if dil > 1:
        h_ref, perm_ref, pos_ref, invf_ref, w_ref, o_ref = refs
        h = _dot(perm_ref[...], h_ref[...]).astype(BF16)
    else:
        h_ref, pos_ref, invf_ref, w_ref, o_ref = refs
        h = h_ref[...]
    half = ROT_DIM // 2
    ang = pos_ref[...].reshape(tm, LANES) * invf_ref[...]
    first_half = lax.broadcasted_iota(jnp.int32, ang.shape, 1) < half
    cos = jnp.cos(ang)
    sin = jnp.where(first_half, -jnp.sin(ang), jnp.sin(ang))
    qscale = ATT_HEAD_DIM ** -0.5
    rot = ((cos * qscale, sin * qscale), (cos, sin))
    for cb in range(3 * ATT_WIDTH // MXU_COLS):
        kind = cb * MXU_COLS // ATT_WIDTH
        acc = _dot(h, w_ref[:, cb * MXU_COLS:(cb + 1) * MXU_COLS])
        for hh in range(MXU_COLS // LANES):
            blk = acc[:, hh * LANES:(hh + 1) * LANES]
            if kind < 2:
                c, s = rot[kind]
                partner = jnp.where(first_half, pltpu.roll(blk, LANES - half, 1),
                                    pltpu.roll(blk, half, 1))
                blk = blk * c + partner * s
            lo = cb * MXU_COLS + hh * LANES
            o_ref[:, :, lo:lo + LANES] = blk.astype(BF16).reshape(dil, tm // dil, LANES)


def _att_proj(h, pos, w, group, dil, tm=512):
    bsz, s, d = h.shape
    tn = 3 * ATT_WIDTH
    half = ROT_DIM // 2
    invf = np.power(np.float32(ROPE_THETA),
                    -np.arange(half, dtype=np.float32) * np.float32(2.0) / np.float32(ROT_DIM))
    invf_full = np.zeros((1, LANES), np.float32)
    invf_full[0, :half] = invf
    invf_full[0, half:ROT_DIM] = invf
    in_specs = [pl.BlockSpec((None, tm, d), lambda b, i: (b, i, 0))]
    operands = [h]
    if dil > 1:
        in_specs.append(pl.BlockSpec((tm, tm), lambda b, i: (0, 0)))
        operands.append(jnp.asarray(_perm_matrix(tm, dil), BF16))
    in_specs += [
        pl.BlockSpec((None, dil, tm // dil, LANES), lambda b, i: (b, 0, i, 0)),
        pl.BlockSpec((1, LANES), lambda b, i: (0, 0)),
        pl.BlockSpec((d, tn), lambda b, i: (0, group)),
    ]
    operands += [pos, jnp.asarray(invf_full), w]
    return pl.pallas_call(
        functools.partial(_att_proj_kernel, dil=dil, tm=tm),
        grid=(bsz, s // tm),
        in_specs=in_specs,
        out_specs=pl.BlockSpec((None, dil, tm // dil, tn), lambda b, i: (b, 0, i, 0)),
        out_shape=jax.ShapeDtypeStruct((bsz, dil, s // dil, tn), BF16),
        compiler_params=_params("parallel", "parallel"),
        name=f"att_proj_g{group}",
    )(*operands)


def _att_kernel(q_ref, kc_ref, kp_ref, vc_ref, vp_ref, o_ref, st_ref, *, tq):
    n = pl.program_id(2)
    blk = ATT_BLOCK
    row = lax.broadcasted_iota(jnp.int32, (blk, 2 * blk), 0)
    col = lax.broadcasted_iota(jnp.int32, (blk, 2 * blk), 1)
    band = (col >= row) & (col <= row + blk)
    band_first = band & ((col >= blk) | (n > 0))
    lane = lax.broadcasted_iota(jnp.int32, (blk, LANES), 1)
    for i in range(tq // blk):
        rows = slice(i * blk, (i + 1) * blk)
        stats = jnp.zeros((blk, LANES), F32)
        for h in range(ATT_HEADS):
            hs = slice(h * ATT_HEAD_DIM, (h + 1) * ATT_HEAD_DIM)
            qi = q_ref[rows, hs]
            if i == 0:
                kk = jnp.concatenate([kp_ref[:, hs], kc_ref[0:blk, hs]], axis=0)
                vv = jnp.concatenate([vp_ref[:, hs], vc_ref[0:blk, hs]], axis=0)
                mask = band_first
            else:
                kk = kc_ref[(i - 1) * blk:(i + 1) * blk, hs]
                vv = vc_ref[(i - 1) * blk:(i + 1) * blk, hs]
                mask = band
            s = jnp.where(mask, _dot_nt(qi, kk), -jnp.inf)
            m = jnp.max(s, axis=-1, keepdims=True)
            p = jnp.exp(s - m)
            l = jnp.sum(p, axis=-1, keepdims=True)
            o = _dot(p.astype(BF16), vv) / l
            o_ref[rows, hs] = o.astype(BF16)
            stats = jnp.where(lane == h, m + jnp.log(l), stats)
        hi = stats.astype(BF16)
        r1 = stats - hi.astype(F32)
        mid = r1.astype(BF16)
        lo = (r1 - mid.astype(F32)).astype(BF16)
        st_ref[rows, 0:LANES] = hi
        st_ref[rows, LANES:2 * LANES] = mid
        st_ref[rows, 2 * LANES:3 * LANES] = lo


def _attention(qkv, tq=256):
    bsz, dil, ln, _ = qkv.shape
    tq = min(tq, ln)
    per = tq // ATT_BLOCK
    cur = lambda c: pl.BlockSpec((None, None, tq, ATT_WIDTH), lambda b, r, n: (b, r, n, c))
    prev = lambda c: pl.BlockSpec((None, None, ATT_BLOCK, ATT_WIDTH),
                                  lambda b, r, n: (b, r, jnp.maximum(n * per - 1, 0), c))
    return pl.pallas_call(
        functools.partial(_att_kernel, tq=tq),
        grid=(bsz, dil, ln // tq),
        in_specs=[cur(0), cur(1), prev(1), cur(2), prev(2)],
        out_specs=[pl.BlockSpec((None, None, tq, ATT_WIDTH), lambda b, r, n: (b, r, n, 0)),
                   pl.BlockSpec((None, None, tq, 3 * LANES), lambda b, r, n: (b, r, n, 0))],
        out_shape=[jax.ShapeDtypeStruct((bsz, dil, ln, ATT_WIDTH), BF16),
                   jax.ShapeDtypeStruct((bsz, dil, ln, 3 * LANES), BF16)],
        compiler_params=_params("parallel", "parallel", "parallel"),
        name=f"attention_d{dil}",
    )(qkv, qkv, qkv, qkv, qkv)


def _att_merge_kernel(o1_ref, o4_ref, o16_ref, s1_ref, s4_ref, s16_ref, pt4_ref, pt16_ref,
                      ex_ref, w_ref, x_ref, gate_ref, ngain_ref, nshift_ref, nscale_ref,
                      y_ref, h_ref, lhs_ref, *, tm):
    def lse_of(pieces):
        return pieces[:, 0:LANES] + pieces[:, LANES:2 * LANES] + pieces[:, 2 * LANES:3 * LANES]

    lse1 = lse_of(s1_ref[...].astype(F32))
    lse4 = lse_of(_dot(pt4_ref[...], s4_ref[...].reshape(tm, 3 * LANES)))
    lse16 = lse_of(_dot(pt16_ref[...], s16_ref[...].reshape(tm, 3 * LANES)))
    m = jnp.maximum(jnp.maximum(lse1, lse4), lse16)
    e1 = jnp.exp(lse1 - m)
    e4 = jnp.exp(lse4 - m)
    e16 = jnp.exp(lse16 - m)
    inv = 1.0 / (e1 + e4 + e16)

    def split(w):
        hi = w.astype(BF16)
        lo = (w - hi.astype(F32)).astype(BF16)
        return jnp.concatenate([hi, lo], axis=1)

    w1, w4, w16 = split(e1 * inv), split(e4 * inv), split(e16 * inv)
    cw = 4 * ATT_HEAD_DIM
    for cb in range(ATT_WIDTH // cw):
        cs = slice(cb * cw, (cb + 1) * cw)
        ex = ex_ref[:, cs]
        a1 = o1_ref[:, cs].astype(F32)
        a4 = _dot(pt4_ref[...], o4_ref[:, :, cs].reshape(tm, cw))
        a16 = _dot(pt16_ref[...], o16_ref[:, :, cs].reshape(tm, cw))
        mix = _dot(w1, ex) * a1 + _dot(w4, ex) * a4 + _dot(w16, ex) * a16
        lhs_ref[:, cs] = mix.astype(BF16)
    ms = _gated_residual_matmul(lhs_ref[...], w_ref, x_ref, gate_ref, y_ref)
    _emit_next_input(y_ref, ms, ngain_ref, nshift_ref, nscale_ref, h_ref)


def _att_merge(outs, stats, w, x, mod, layer, mlp_gains, tm=256):
    bsz, s, d = x.shape
    in_specs, operands = [], []
    for arr in list(outs) + list(stats):
        dil, width = arr.shape[1], arr.shape[3]
        if dil == 1:
            in_specs.append(pl.BlockSpec((None, None, tm, width), lambda b, i: (b, 0, i, 0)))
        else:
            in_specs.append(pl.BlockSpec((None, dil, tm // dil, width), lambda b, i: (b, 0, i, 0)))
        operands.append(arr)
    expand = np.zeros((2 * LANES, ATT_WIDTH), np.float32)
    for h in range(ATT_HEADS):
        expand[h, h * ATT_HEAD_DIM:(h + 1) * ATT_HEAD_DIM] = 1.0
        expand[LANES + h, h * ATT_HEAD_DIM:(h + 1) * ATT_HEAD_DIM] = 1.0
    const2 = lambda b, i: (0, 0)
    tile = lambda b, i: (b, i, 0)
    in_specs += [
        pl.BlockSpec((tm, tm), const2),
        pl.BlockSpec((tm, tm), const2),
        pl.BlockSpec((2 * LANES, ATT_WIDTH), const2),
        pl.BlockSpec((ATT_WIDTH, d), const2),
        pl.BlockSpec((None, tm, d), tile),
        _mod_spec(layer, 2),
        _row_spec(layer),
        _mod_spec(layer, 3),
        _mod_spec(layer, 4),
    ]
    operands += [jnp.asarray(_perm_matrix(tm, 4).T, BF16), jnp.asarray(_perm_matrix(tm, 16).T, BF16),
                 jnp.asarray(expand, BF16), w, x, mod, mlp_gains, mod, mod]
    return pl.pallas_call(
        functools.partial(_att_merge_kernel, tm=tm),
        grid=(bsz, s // tm),
        in_specs=in_specs,
        out_specs=[pl.BlockSpec((None, tm, d), tile), pl.BlockSpec((None, tm, d), tile)],
        out_shape=[jax.ShapeDtypeStruct((bsz, s, d), F32), jax.ShapeDtypeStruct((bsz, s, d), BF16)],
        scratch_shapes=[pltpu.VMEM((tm, ATT_WIDTH), BF16)],
        compiler_params=_params("parallel", "parallel"),
        name="att_merge",
    )(*operands)


def kernel(x, c, positions, ada_w, ada_b, norm_mix, norm_mlp, ret_w_in, ret_w_out,
           att_w_in, att_w_out, mlp_w1, mlp_w2, final_norm):
    bsz, s, d = x.shape
    depth = ada_w.shape[0]
    assert d == D_MODEL and depth == 2 and s % 2048 == 0
    mod = _ada(c, ada_w, ada_b).reshape(depth, bsz, 6, 1, d)
    mix_gains = norm_mix.reshape(depth, 1, d)
    mlp_gains = norm_mlp.reshape(depth, 1, d)
    w1 = mlp_w1.astype(BF16)
    w2 = mlp_w2.astype(BF16)
    lane_pos = lambda p: jnp.broadcast_to(p.astype(F32)[..., None], p.shape + (LANES,))

    proj = _ret_proj(x, lane_pos(positions), mod, 0, mix_gains, ret_w_in[0].astype(BF16))
    ret = _retention(proj)
    x, h = _out_proj(ret, proj, ret_w_out[0].astype(BF16), x, mod, 0, mlp_gains)
    x, h = _mlp_down(_mlp_up(h, w1, 0), w2, x, mod, 0, mix_gains=mix_gains)

    w_att = att_w_in[0].astype(BF16)
    outs, stats = [], []
    for gi, (_, dil) in enumerate(ATT_GROUPS):
        pos_g = lane_pos(positions.reshape(bsz, s // dil, dil).transpose(0, 2, 1))
        o_g, st_g = _attention(_att_proj(h, pos_g, w_att, gi, dil))
        outs.append(o_g)
        stats.append(st_g)
    x, h = _att_merge(outs, stats, att_w_out[0].astype(BF16), x, mod, 1, mlp_gains)
    return _mlp_down(_mlp_up(h, w1, 1), w2, x, mod, 1, fgain=final_norm.reshape(1, d))
```

```python
import functools

import numpy as np
import jax
import jax.numpy as jnp
from jax import lax
from jax.experimental import pallas as pl
from jax.experimental.pallas import tpu as pltpu

F32 = jnp.float32
BF16 = jnp.bfloat16

EPS = 1e-6
D_MODEL = 2048
D_FF = 4 * D_MODEL
RET_HEADS = 8
RET_DK = D_MODEL // RET_HEADS
RET_DV = 2 * RET_DK
RET_QK = RET_HEADS * RET_DK
RET_V = RET_HEADS * RET_DV
RET_PROJ = 2 * RET_QK + 2 * RET_V
RET_THETA = 10000.0
RET_CHUNK = 256
ATT_HEADS = 16
ATT_HEAD_DIM = D_MODEL // ATT_HEADS
ATT_WIDTH = ATT_HEADS * ATT_HEAD_DIM
ATT_GROUPS = ((128, 1), (512, 4), (2048, 16))
ATT_BLOCK = 128
ROT_DIM = ATT_HEAD_DIM // 4
ROPE_THETA = 500000.0
LANES = 128
MXU_COLS = 256

VMEM_LIMIT_BYTES = 60 * 1024 * 1024


def _params(*sem):
    return pltpu.CompilerParams(dimension_semantics=sem, vmem_limit_bytes=VMEM_LIMIT_BYTES)


def _dot(a, b):
    return jnp.dot(a, b, preferred_element_type=F32)


def _dot_nt(a, b):
    return lax.dot_general(a, b, (((1,), (1,)), ((), ())), preferred_element_type=F32)


def _dot_tn(a, b):
    return lax.dot_general(a, b, (((0,), (0,)), ((), ())), preferred_element_type=F32)


def _norm_modulate(x, gain, shift, scale):
    ms = jnp.mean(x * x, axis=-1, keepdims=True)
    y = x * lax.rsqrt(ms + EPS) * gain
    return y * (1.0 + scale) + shift


def _gated_residual_matmul(a, w_ref, x_ref, gate_ref, y_ref):
    d = y_ref.shape[-1]
    ssq = jnp.zeros((y_ref.shape[0], 1), F32)
    for c in range(d // MXU_COLS):
        cs = slice(c * MXU_COLS, (c + 1) * MXU_COLS)
        y = x_ref[:, cs] + gate_ref[:, cs] * _dot(a, w_ref[:, cs])
        y_ref[:, cs] = y
        ssq = ssq + jnp.sum(y * y, axis=-1, keepdims=True)
    return ssq * (1.0 / d)


def _ada_kernel(ct_ref, w_ref, b_ref, o_ref):
    ct = ct_ref[...]
    cs = ct * jax.nn.sigmoid(ct)
    w = w_ref[...]
    for b in range(o_ref.shape[0]):
        o_ref[b:b + 1, :] = jnp.sum(w * cs[:, b:b + 1], axis=0, keepdims=True) + b_ref[...]


def _ada(c, ada_w, ada_b, tn=1024):
    depth, d, n = ada_w.shape
    bsz = c.shape[0]
    return pl.pallas_call(
        _ada_kernel,
        grid=(depth, n // tn),
        in_specs=[
            pl.BlockSpec((d, bsz), lambda l, j: (0, 0)),
            pl.BlockSpec((None, d, tn), lambda l, j: (l, 0, j)),
            pl.BlockSpec((None, 1, tn), lambda l, j: (l, 0, j)),
        ],
        out_specs=pl.BlockSpec((None, bsz, tn), lambda l, j: (l, 0, j)),
        out_shape=jax.ShapeDtypeStruct((depth, bsz, n), F32),
        compiler_params=_params("parallel", "parallel"),
        name="ada_mod",
    )(c.T, ada_w, ada_b.reshape(depth, 1, n))


def _mod_spec(layer, which):
    return pl.BlockSpec((None, None, None, 1, D_MODEL),
                        lambda b, *_: (layer, b, which, 0, 0))


def _row_spec(layer):
    return pl.BlockSpec((None, 1, D_MODEL), lambda *_: (layer, 0, 0))


def _ret_qk_kernel(x_ref, pos_ref, gain_ref, shift_ref, scale_ref, invf_ref, w_ref,
                   o_ref, h_ref, *, tm):
    half = RET_DK // 2
    ang = pos_ref[...] * invf_ref[...]
    cos = jnp.cos(ang)
    sin = jnp.sin(ang)
    kscale = RET_DK ** -0.5
    rot = ((cos, sin), (cos * kscale, sin * kscale))
    gain, shift, scale = gain_ref[...], shift_ref[...], scale_ref[...]
    hm = tm // 2
    for r in range(2):
        rows = slice(r * hm, (r + 1) * hm)
        h = _norm_modulate(x_ref[rows, :], gain, shift, scale).astype(BF16)
        h_ref[rows, :] = h
        for hh in range(2 * RET_HEADS):
            c, s = rot[hh // RET_HEADS]
            c, s = c[rows], s[rows]
            lo = hh * RET_DK
            acc = _dot(h, w_ref[:, lo:lo + RET_DK])
            a1 = acc[:, :half]
            a2 = acc[:, half:]
            o_ref[rows, lo:lo + half] = (a1 * c - a2 * s).astype(BF16)
            o_ref[rows, lo + half:lo + RET_DK] = (a2 * c + a1 * s).astype(BF16)


def _ret_qk(x, pos, mod, layer, gains, w, tm=512):
    bsz, s, d = x.shape
    n = 2 * RET_QK
    half = RET_DK // 2
    invf = np.power(np.float32(RET_THETA),
                    -np.arange(half, dtype=np.float32) * np.float32(2.0) / np.float32(RET_DK))
    invf = jnp.asarray(invf.astype(np.float32).reshape(1, half))
    tile = lambda b, i: (b, i, 0)
    return pl.pallas_call(
        functools.partial(_ret_qk_kernel, tm=tm),
        grid=(bsz, s // tm),
        in_specs=[
            pl.BlockSpec((None, tm, d), tile),
            pl.BlockSpec((None, tm, LANES), tile),
            _row_spec(layer),
            _mod_spec(layer, 0),
            _mod_spec(layer, 1),
            pl.BlockSpec((1, half), lambda b, i: (0, 0)),
            pl.BlockSpec((d, n), lambda b, i: (0, 0)),
        ],
        out_specs=[pl.BlockSpec((None, tm, n), tile), pl.BlockSpec((None, tm, d), tile)],
        out_shape=[jax.ShapeDtypeStruct((bsz, s, n), BF16), jax.ShapeDtypeStruct((bsz, s, d), BF16)],
        compiler_params=_params("parallel", "parallel"),
        name="ret_qk",
    )(x, pos, gains, mod, mod, invf, w)


def _matmul_kernel(h_ref, w_ref, o_ref):
    h = h_ref[...]
    cw = 2 * MXU_COLS
    for c in range(o_ref.shape[-1] // cw):
        cs = slice(c * cw, (c + 1) * cw)
        o_ref[:, cs] = _dot(h, w_ref[:, cs]).astype(BF16)


def _ret_vg(h, w, tm=256):
    bsz, s, d = h.shape
    n = 2 * RET_V
    tile = lambda b, i: (b, i, 0)
    return pl.pallas_call(
        _matmul_kernel,
        grid=(bsz, s // tm),
        in_specs=[pl.BlockSpec((None, tm, d), tile),
                  pl.BlockSpec((d, n), lambda b, i: (0, 0))],
        out_specs=pl.BlockSpec((None, tm, n), tile),
        out_shape=jax.ShapeDtypeStruct((bsz, s, n), BF16),
        compiler_params=_params("parallel", "parallel"),
        name="ret_vg",
    )(h, w)


def _ret_tables(chunk):
    lg = np.log1p(-np.exp2(-5.0 - np.arange(RET_HEADS, dtype=np.float64)))
    idx = np.arange(chunk, dtype=np.float64)
    diff = idx[:, None] - idx[None, :]
    decay = np.where(diff[None] >= 0, np.exp(np.maximum(diff, 0.0)[None] * lg[:, None, None]), 0.0)
    xi = np.exp((idx + 1.0)[None, :] * lg[:, None])
    zeta = np.exp((chunk - 1.0 - idx)[None, :] * lg[:, None])
    cd = np.exp(chunk * lg)
    return (jnp.asarray(decay.astype(np.float32)),
            jnp.asarray(xi.astype(np.float32)[:, :, None]),
            jnp.asarray(zeta.astype(np.float32)[:, :, None]),
            [float(v) for v in cd])


def _ret_kernel(q_ref, k_ref, v_ref, decay_ref, xi_ref, zeta_ref, o_ref, state_ref,
                *, tt, chunk, cd):
    @pl.when(pl.program_id(1) == 0)
    def _():
        state_ref[...] = jnp.zeros_like(state_ref)

    for h in range(RET_HEADS):
        qs = slice(h * RET_DK, (h + 1) * RET_DK)
        vs = slice(h * RET_DV, (h + 1) * RET_DV)
        for c in range(tt // chunk):
            rows = slice(c * chunk, (c + 1) * chunk)
            qc = q_ref[rows, qs]
            kc = k_ref[rows, qs]
            vc = v_ref[rows, vs]
            st = state_ref[h]
            scores = _dot_nt(qc, kc) * decay_ref[h]
            inner = _dot(scores.astype(BF16), vc)
            cross = _dot(qc, st.astype(BF16)) * xi_ref[h]
            o_ref[rows, vs] = (inner + cross).astype(BF16)
            kz = (kc.astype(F32) * zeta_ref[h]).astype(BF16)
            state_ref[h] = cd[h] * st + _dot_tn(kz, vc)


def _retention(qk, vg, tt=512, chunk=RET_CHUNK):
    bsz, s, _ = qk.shape
    decay, xi, zeta, cd = _ret_tables(chunk)
    const3 = lambda b, n: (0, 0, 0)
    return pl.pallas_call(
        functools.partial(_ret_kernel, tt=tt, chunk=chunk, cd=cd),
        grid=(bsz, s // tt),
        in_specs=[
            pl.BlockSpec((None, tt, RET_QK), lambda b, n: (b, n, 0)),
            pl.BlockSpec((None, tt, RET_QK), lambda b, n: (b, n, 1)),
            pl.BlockSpec((None, tt, RET_V), lambda b, n: (b, n, 0)),
            pl.BlockSpec((RET_HEADS, chunk, chunk), const3),
            pl.BlockSpec((RET_HEADS, chunk, 1), const3),
            pl.BlockSpec((RET_HEADS, chunk, 1), const3),
        ],
        out_specs=pl.BlockSpec((None, tt, RET_V), lambda b, n: (b, n, 0)),
        out_shape=jax.ShapeDtypeStruct((bsz, s, RET_V), BF16),
        scratch_shapes=[pltpu.VMEM((RET_HEADS, RET_DK, RET_DV), F32)],
        compiler_params=_params("parallel", "arbitrary"),
        name="retention",
    )(qk, qk, vg, decay, xi, zeta)


def _emit_next_input(y_ref, ms, gain_ref, shift_ref, scale_ref, h_ref):
    g2 = gain_ref[...] * (1.0 + scale_ref[...])
    h_ref[...] = (y_ref[...] * lax.rsqrt(ms + EPS) * g2 + shift_ref[...]).astype(BF16)


def _gated_head(o_ref, g_ref, h):
    vs = slice(h * RET_DV, (h + 1) * RET_DV)
    o = o_ref[:, vs].astype(F32)
    mu = jnp.mean(o, axis=-1, keepdims=True)
    oc = o - mu
    var = jnp.mean(oc * oc, axis=-1, keepdims=True)
    g = g_ref[:, vs].astype(F32)
    return (g * jax.nn.sigmoid(g) * (oc * lax.rsqrt(var + EPS))).astype(BF16)


def _out_proj_kernel(o_ref, g_ref, w_ref, x_ref, gate_ref, ngain_ref, nshift_ref, nscale_ref,
                     y_ref, h_ref):
    last = RET_HEADS - 1
    for h in range(last):
        contrib = _dot(_gated_head(o_ref, g_ref, h), w_ref[h * RET_DV:(h + 1) * RET_DV, :])
        if h == 0:
            y_ref[...] = contrib
        else:
            y_ref[...] += contrib
    d = y_ref.shape[-1]
    a = _gated_head(o_ref, g_ref, last)
    ssq = jnp.zeros((y_ref.shape[0], 1), F32)
    for c in range(d // MXU_COLS):
        cs = slice(c * MXU_COLS, (c + 1) * MXU_COLS)
        acc = y_ref[:, cs] + _dot(a, w_ref[last * RET_DV:, cs])
        y = x_ref[:, cs] + gate_ref[:, cs] * acc
        y_ref[:, cs] = y
        ssq = ssq + jnp.sum(y * y, axis=-1, keepdims=True)
    _emit_next_input(y_ref, ssq * (1.0 / d), ngain_ref, nshift_ref, nscale_ref, h_ref)


def _out_proj(o, proj, w, x, mod, layer, mlp_gains, tm=256):
    bsz, s, k = o.shape
    d = w.shape[1]
    tile = lambda b, i: (b, i, 0)
    return pl.pallas_call(
        _out_proj_kernel,
        grid=(bsz, s // tm),
        in_specs=[
            pl.BlockSpec((None, tm, k), tile),
            pl.BlockSpec((None, tm, k), lambda b, i: (b, i, 1)),
            pl.BlockSpec((k, d), lambda b, i: (0, 0)),
            pl.BlockSpec((None, tm, d), tile),
            _mod_spec(layer, 2),
            _row_spec(layer),
            _mod_spec(layer, 3),
            _mod_spec(layer, 4),
        ],
        out_specs=[pl.BlockSpec((None, tm, d), tile), pl.BlockSpec((None, tm, d), tile)],
        out_shape=[jax.ShapeDtypeStruct((bsz, s, d), F32), jax.ShapeDtypeStruct((bsz, s, d), BF16)],
        compiler_params=_params("parallel", "parallel"),
        name="out_proj",
    )(o, proj, w, x, mod, mlp_gains, mod, mod)


def _mlp_up_kernel(h_ref, w_ref, o_ref):
    h = h_ref[...]
    cw = 2 * MXU_COLS
    for c in range(o_ref.shape[-1] // cw):
        cs = slice(c * cw, (c + 1) * cw)
        a = jnp.maximum(_dot(h, w_ref[:, cs]), 0.0)
        o_ref[:, cs] = (a * a).astype(BF16)


def _mlp_up(h, w1, layer, tm=256):
    bsz, s, d = h.shape
    dff = w1.shape[2]
    tile = lambda b, i: (b, i, 0)
    return pl.pallas_call(
        _mlp_up_kernel,
        grid=(bsz, s // tm),
        in_specs=[pl.BlockSpec((None, tm, d), tile),
                  pl.BlockSpec((None, d, dff), lambda b, i: (layer, 0, 0))],
        out_specs=pl.BlockSpec((None, tm, dff), tile),
        out_shape=jax.ShapeDtypeStruct((bsz, s, dff), BF16),
        compiler_params=_params("parallel", "parallel"),
        name="mlp_up",
    )(h, w1)


def _mlp_down_kernel(*refs, final_norm):
    if final_norm:
        a_ref, w_ref, x_ref, gate_ref, fgain_ref, y_ref = refs
    else:
        a_ref, w_ref, x_ref, gate_ref, ngain_ref, nshift_ref, nscale_ref, y_ref, h_ref = refs
    ms = _gated_residual_matmul(a_ref[...], w_ref, x_ref, gate_ref, y_ref)
    if final_norm:
        y_ref[...] = y_ref[...] * lax.rsqrt(ms + EPS) * fgain_ref[...]
    else:
        _emit_next_input(y_ref, ms, ngain_ref, nshift_ref, nscale_ref, h_ref)


def _mlp_down(a, w2, x, mod, layer, fgain=None, mix_gains=None, tm=256):
    bsz, s, dff = a.shape
    d = x.shape[-1]
    final_norm = fgain is not None
    tile = lambda b, i: (b, i, 0)
    in_specs = [
        pl.BlockSpec((None, tm, dff), tile),
        pl.BlockSpec((None, dff, d), lambda b, i: (layer, 0, 0)),
        pl.BlockSpec((None, tm, d), tile),
        _mod_spec(layer, 5),
    ]
    operands = [a, w2, x, mod]
    if final_norm:
        in_specs.append(pl.BlockSpec((1, d), lambda b, i: (0, 0)))
        operands.append(fgain)
        out_specs = pl.BlockSpec((None, tm, d), tile)
        out_shape = jax.ShapeDtypeStruct((bsz, s, d), F32)
    else:
        in_specs += [_row_spec(layer + 1), _mod_spec(layer + 1, 0), _mod_spec(layer + 1, 1)]
        operands += [mix_gains, mod, mod]
        out_specs = [pl.BlockSpec((None, tm, d), tile), pl.BlockSpec((None, tm, d), tile)]
        out_shape = [jax.ShapeDtypeStruct((bsz, s, d), F32), jax.ShapeDtypeStruct((bsz, s, d), BF16)]
    return pl.pallas_call(
        functools.partial(_mlp_down_kernel, final_norm=final_norm),
        grid=(bsz, s // tm),
        in_specs=in_specs,
        out_specs=out_specs,
        out_shape=out_shape,
        compiler_params=_params("parallel", "parallel"),
        name="mlp_down",
    )(*operands)


def _perm_matrix(tm, dil):
    t = np.arange(tm)
    p = (t % dil) * (tm // dil) + t // dil
    m = np.zeros((tm, tm), np.float32)
    m[p, t] = 1.0
    return m


def _att_proj_kernel(*refs, dil, tm, sub):
    n_sub = tm // sub
    sd = sub // dil
    if dil > 1:
        h_ref, perm_ref, pos_ref, invf_ref, w_ref, o_ref = refs
        h = jnp.concatenate(
            [_dot(perm_ref[...], h_ref[u * sub:(u + 1) * sub, :]).astype(BF16) for u in range(n_sub)],
            axis=0)
    else:
        h_ref, pos_ref, invf_ref, w_ref, o_ref = refs
        h = h_ref[...]
    half = ROT_DIM // 2
    pos = jnp.concatenate(
        [pos_ref[:, u * sd:(u + 1) * sd, :].reshape(sub, LANES) for u in range(n_sub)], axis=0)
    ang = pos * invf_ref[...]
    first_half = lax.broadcasted_iota(jnp.int32, ang.shape, 1) < half
    cos = jnp.cos(ang)
    sin = jnp.where(first_half, -jnp.sin(ang), jnp.sin(ang))
    qscale = ATT_HEAD_DIM ** -0.5
    rot = ((cos * qscale, sin * qscale), (cos, sin))
    for cb in range(3 * ATT_WIDTH // MXU_COLS):
        kind = cb * MXU_COLS // ATT_WIDTH
        acc = _dot(h, w_ref[:, cb * MXU_COLS:(cb + 1) * MXU_COLS])
        for hh in range(MXU_COLS // LANES):
            blk = acc[:, hh * LANES:(hh + 1) * LANES]
            if kind < 2:
                c, s = rot[kind]
                partner = jnp.where(first_half, pltpu.roll(blk, LANES - half, 1),
                                    pltpu.roll(blk, half, 1))
                blk = blk * c + partner * s
            lo = cb * MXU_COLS + hh * LANES
            blk = blk.astype(BF16)
            for u in range(n_sub):
                o_ref[:, u * sd:(u + 1) * sd, lo:lo + LANES] = (
                    blk[u * sub:(u + 1) * sub].reshape(dil, sd, LANES))


def _att_proj(h, pos, w, group, dil, tm=512):
    bsz, s, d = h.shape
    tn = 3 * ATT_WIDTH
    half = ROT_DIM // 2
    invf = np.power(np.float32(ROPE_THETA),
                    -np.arange(half, dtype=np.float32) * np.float32(2.0) / np.float32(ROT_DIM))
    invf_full = np.zeros((1, LANES), np.float32)
    invf_full[0, :half] = invf
    invf_full[0, half:ROT_DIM] = invf
    sub = tm if dil == 1 else tm // 2
    in_specs = [pl.BlockSpec((None, tm, d), lambda b, i: (b, i, 0))]
    operands = [h]
    if dil > 1:
        in_specs.append(pl.BlockSpec((sub, sub), lambda b, i: (0, 0)))
        operands.append(jnp.asarray(_perm_matrix(sub, dil), BF16))
    in_specs += [
        pl.BlockSpec((None, dil, tm // dil, LANES), lambda b, i: (b, 0, i, 0)),
        pl.BlockSpec((1, LANES), lambda b, i: (0, 0)),
        pl.BlockSpec((d, tn), lambda b, i: (0, group)),
    ]
    operands += [pos, jnp.asarray(invf_full), w]
    return pl.pallas_call(
        functools.partial(_att_proj_kernel, dil=dil, tm=tm, sub=sub),
        grid=(bsz, s // tm),
        in_specs=in_specs,
        out_specs=pl.BlockSpec((None, dil, tm // dil, tn), lambda b, i: (b, 0, i, 0)),
        out_shape=jax.ShapeDtypeStruct((bsz, dil, s // dil, tn), BF16),
        compiler_params=_params("parallel", "parallel"),
        name=f"att_proj_g{group}",
    )(*operands)


def _att_kernel(q_ref, kc_ref, kp_ref, vc_ref, vp_ref, o_ref, st_ref, *, tq):
    n = pl.program_id(2)
    blk = ATT_BLOCK
    row = lax.broadcasted_iota(jnp.int32, (blk, 2 * blk), 0)
    col = lax.broadcasted_iota(jnp.int32, (blk, 2 * blk), 1)
    band = (col >= row) & (col <= row + blk)
    band_first = band & ((col >= blk) | (n > 0))
    lane = lax.broadcasted_iota(jnp.int32, (blk, LANES), 1)
    for i in range(tq // blk):
        rows = slice(i * blk, (i + 1) * blk)
        stats = jnp.zeros((blk, LANES), F32)
        for h in range(ATT_HEADS):
            hs = slice(h * ATT_HEAD_DIM, (h + 1) * ATT_HEAD_DIM)
            qi = q_ref[rows, hs]
            if i == 0:
                kk = jnp.concatenate([kp_ref[:, hs], kc_ref[0:blk, hs]], axis=0)
                vv = jnp.concatenate([vp_ref[:, hs], vc_ref[0:blk, hs]], axis=0)
                mask = band_first
            else:
                kk = kc_ref[(i - 1) * blk:(i + 1) * blk, hs]
                vv = vc_ref[(i - 1) * blk:(i + 1) * blk, hs]
                mask = band
            s = jnp.where(mask, _dot_nt(qi, kk), -jnp.inf)
            m = jnp.max(s, axis=-1, keepdims=True)
            p = jnp.exp(s - m)
            l = jnp.sum(p, axis=-1, keepdims=True)
            o = _dot(p.astype(BF16), vv) / l
            o_ref[rows, hs] = o.astype(BF16)
            stats = jnp.where(lane == h, m + jnp.log(l), stats)
        hi = stats.astype(BF16)
        r1 = stats - hi.astype(F32)
        mid = r1.astype(BF16)
        lo = (r1 - mid.astype(F32)).astype(BF16)
        st_ref[rows, 0:LANES] = hi
        st_ref[rows, LANES:2 * LANES] = mid
        st_ref[rows, 2 * LANES:3 * LANES] = lo


def _attention(qkv, tq=256):
    bsz, dil, ln, _ = qkv.shape
    tq = min(tq, ln)
    per = tq // ATT_BLOCK
    cur = lambda c: pl.BlockSpec((None, None, tq, ATT_WIDTH), lambda b, r, n: (b, r, n, c))
    prev = lambda c: pl.BlockSpec((None, None, ATT_BLOCK, ATT_WIDTH),
                                  lambda b, r, n: (b, r, jnp.maximum(n * per - 1, 0), c))
    return pl.pallas_call(
        functools.partial(_att_kernel, tq=tq),
        grid=(bsz, dil, ln // tq),
        in_specs=[cur(0), cur(1), prev(1), cur(2), prev(2)],
        out_specs=[pl.BlockSpec((None, None, tq, ATT_WIDTH), lambda b, r, n: (b, r, n, 0)),
                   pl.BlockSpec((None, None, tq, 3 * LANES), lambda b, r, n: (b, r, n, 0))],
        out_shape=[jax.ShapeDtypeStruct((bsz, dil, ln, ATT_WIDTH), BF16),
                   jax.ShapeDtypeStruct((bsz, dil, ln, 3 * LANES), BF16)],
        compiler_params=_params("parallel", "parallel", "parallel"),
        name=f"attention_d{dil}",
    )(qkv, qkv, qkv, qkv, qkv)


def _att_merge_kernel(o1_ref, o4_ref, o16_ref, s1_ref, s4_ref, s16_ref, pt4_ref, pt16_ref,
                      ex_ref, w_ref, x_ref, gate_ref, ngain_ref, nshift_ref, nscale_ref,
                      y_ref, h_ref, lhs_ref, *, tm):
    def lse_of(pieces):
        return pieces[:, 0:LANES] + pieces[:, LANES:2 * LANES] + pieces[:, 2 * LANES:3 * LANES]

    lse1 = lse_of(s1_ref[...].astype(F32))
    lse4 = lse_of(_dot(pt4_ref[...], s4_ref[...].reshape(tm, 3 * LANES)))
    lse16 = lse_of(_dot(pt16_ref[...], s16_ref[...].reshape(tm, 3 * LANES)))
    m = jnp.maximum(jnp.maximum(lse1, lse4), lse16)
    e1 = jnp.exp(lse1 - m)
    e4 = jnp.exp(lse4 - m)
    e16 = jnp.exp(lse16 - m)
    inv = 1.0 / (e1 + e4 + e16)

    def split(w):
        hi = w.astype(BF16)
        lo = (w - hi.astype(F32)).astype(BF16)
        return jnp.concatenate([hi, lo], axis=1)

    w1, w4, w16 = split(e1 * inv), split(e4 * inv), split(e16 * inv)
    cw = 4 * ATT_HEAD_DIM
    for cb in range(ATT_WIDTH // cw):
        cs = slice(cb * cw, (cb + 1) * cw)
        ex = ex_ref[:, cs]
        a1 = o1_ref[:, cs].astype(F32)
        a4 = _dot(pt4_ref[...], o4_ref[:, :, cs].reshape(tm, cw))
        a16 = _dot(pt16_ref[...], o16_ref[:, :, cs].reshape(tm, cw))
        mix = _dot(w1, ex) * a1 + _dot(w4, ex) * a4 + _dot(w16, ex) * a16
        lhs_ref[:, cs] = mix.astype(BF16)
    ms = _gated_residual_matmul(lhs_ref[...], w_ref, x_ref, gate_ref, y_ref)
    _emit_next_input(y_ref, ms, ngain_ref, nshift_ref, nscale_ref, h_ref)


def _att_merge(outs, stats, w, x, mod, layer, mlp_gains, tm=256):
    bsz, s, d = x.shape
    in_specs, operands = [], []
    for arr in list(outs) + list(stats):
        dil, width = arr.shape[1], arr.shape[3]
        if dil == 1:
            in_specs.append(pl.BlockSpec((None, None, tm, width), lambda b, i: (b, 0, i, 0)))
        else:
            in_specs.append(pl.BlockSpec((None, dil, tm // dil, width), lambda b, i: (b, 0, i, 0)))
        operands.append(arr)
    expand = np.zeros((2 * LANES, ATT_WIDTH), np.float32)
    for h in range(ATT_HEADS):
        expand[h, h * ATT_HEAD_DIM:(h + 1) * ATT_HEAD_DIM] = 1.0
        expand[LANES + h, h * ATT_HEAD_DIM:(h + 1) * ATT_HEAD_DIM] = 1.0
    const2 = lambda b, i: (0, 0)
    tile = lambda b, i: (b, i, 0)
    in_specs += [
        pl.BlockSpec((tm, tm), const2),
        pl.BlockSpec((tm, tm), const2),
        pl.BlockSpec((2 * LANES, ATT_WIDTH), const2),
        pl.BlockSpec((ATT_WIDTH, d), const2),
        pl.BlockSpec((None, tm, d), tile),
        _mod_spec(layer, 2),
        _row_spec(layer),
        _mod_spec(layer, 3),
        _mod_spec(layer, 4),
    ]
    operands += [jnp.asarray(_perm_matrix(tm, 4).T, BF16), jnp.asarray(_perm_matrix(tm, 16).T, BF16),
                 jnp.asarray(expand, BF16), w, x, mod, mlp_gains, mod, mod]
    return pl.pallas_call(
        functools.partial(_att_merge_kernel, tm=tm),
        grid=(bsz, s // tm),
        in_specs=in_specs,
        out_specs=[pl.BlockSpec((None, tm, d), tile), pl.BlockSpec((None, tm, d), tile)],
        out_shape=[jax.ShapeDtypeStruct((bsz, s, d), F32), jax.ShapeDtypeStruct((bsz, s, d), BF16)],
        scratch_shapes=[pltpu.VMEM((tm, ATT_WIDTH), BF16)],
        compiler_params=_params("parallel", "parallel"),
        name="att_merge",
    )(*operands)


def kernel(x, c, positions, ada_w, ada_b, norm_mix, norm_mlp, ret_w_in, ret_w_out,
           att_w_in, att_w_out, mlp_w1, mlp_w2, final_norm):
    bsz, s, d = x.shape
    depth = ada_w.shape[0]
    assert d == D_MODEL and depth == 2 and s % 2048 == 0
    mod = _ada(c, ada_w, ada_b).reshape(depth, bsz, 6, 1, d)
    mix_gains = norm_mix.reshape(depth, 1, d)
    mlp_gains = norm_mlp.reshape(depth, 1, d)
    w1 = mlp_w1.astype(BF16)
    w2 = mlp_w2.astype(BF16)
    lane_pos = lambda p: jnp.broadcast_to(p.astype(F32)[..., None], p.shape + (LANES,))

    w_qk = ret_w_in[0][:, :2 * RET_QK].astype(BF16)
    w_vg = ret_w_in[0][:, 2 * RET_QK:].astype(BF16)
    qk, h = _ret_qk(x, lane_pos(positions), mod, 0, mix_gains, w_qk)
    vg = _ret_vg(h, w_vg)
    ret = _retention(qk, vg)
    x, h = _out_proj(ret, vg, ret_w_out[0].astype(BF16), x, mod, 0, mlp_gains)
    x, h = _mlp_down(_mlp_up(h, w1, 0), w2, x, mod, 0, mix_gains=mix_gains)

    w_att = att_w_in[0].astype(BF16)
    outs, stats = [], []
    for gi, (_, dil) in enumerate(ATT_GROUPS):
        pos_g = lane_pos(positions.reshape(bsz, s // dil, dil).transpose(0, 2, 1))
        o_g, st_g = _attention(_att_proj(h, pos_g, w_att, gi, dil))
        outs.append(o_g)
        stats.append(st_g)
    x, h = _att_merge(outs, stats, att_w_out[0].astype(BF16), x, mod, 1, mlp_gains)
    return _mlp_down(_mlp_up(h, w1, 1), w2, x, mod, 1, fgain=final_norm.reshape(1, d))
```
